```python
import math
import jax
import jax.numpy as jnp
from jax import lax
import numpy as np

D_MODEL = 4096
BATCH = 4
SEQ = 2048
DEPTH = 1
DEC_BATCH = 128
DEC_SEQ = 1
PAST_LEN = 2048
PAGE_SIZE = 128

N_ATTN_HEADS = 16
N_KV_HEADS = 4
ATTN_GROUP = N_ATTN_HEADS // N_KV_HEADS
ATTN_HD = D_MODEL // N_ATTN_HEADS // 2
ATTN_VD = 2 * ATTN_HD
ATTN_Q_BLOCK = 128
N_BUCKETS = 32
MAX_DISTANCE = 128
GDN_HEADS = 32
GDN_DK = 128
GDN_DV = 128
GDN_CONV = 4
GDN_CHUNK = 64
GDN_QKV = GDN_HEADS * (2 * GDN_DK + GDN_DV)
ATTN_OUT = N_ATTN_HEADS * ATTN_VD
GDN_OUT = GDN_HEADS * GDN_DV
MIX_WIDTH = ATTN_OUT + GDN_OUT
IN_WIDTHS = (N_ATTN_HEADS * 2 * ATTN_HD, N_KV_HEADS * 2 * ATTN_HD, N_KV_HEADS * ATTN_VD,
             GDN_QKV, GDN_OUT, GDN_HEADS, GDN_HEADS, 2 * D_MODEL)
IN_WIDTH = sum(IN_WIDTHS)
N_EXPERTS = 128
TOP_K = 8
N_GROUPS = 8
TOPK_GROUPS = 4
EXPERT_FF = 1024
SHARED_FF = 1024
ROUTED_SCALE = 2.5
MOE_BLOCK = 128
DEEPNORM_ALPHA = (2 * DEPTH) ** 0.25
DEEPNORM_BETA = (8 * DEPTH) ** -0.25

kernel_name = 'hybrid_diffattn_gdn_moe_decode_step'


def layer_norm(x, g, b, eps=1e-5):
    xf = x.astype(jnp.float32)
    mu = jnp.mean(xf, -1, keepdims=True)
    var = jnp.mean(jnp.square(xf - mu), -1, keepdims=True)
    return ((xf - mu) * lax.rsqrt(var + eps) * g.astype(jnp.float32) + b.astype(jnp.float32)).astype(x.dtype)


def rms_norm(x, w, eps):
    xf = x.astype(jnp.float32)
    return xf * lax.rsqrt(jnp.mean(xf * xf, -1, keepdims=True) + eps) * w.astype(jnp.float32)


def l2_normalize(x, eps=1e-6):
    return x * lax.rsqrt(jnp.sum(x * x, -1, keepdims=True) + eps)


def split_cols(a, widths):
    return jnp.split(a, np.cumsum(widths)[:-1].tolist(), axis=-1)


def t5_bucket(rel):
    n = jnp.maximum(rel, 0)
    max_exact = N_BUCKETS // 2
    nf = jnp.maximum(n, 1).astype(jnp.float32)
    large = max_exact + (jnp.log(nf / max_exact) / math.log(MAX_DISTANCE / max_exact)
                         * (N_BUCKETS - max_exact)).astype(jnp.int32)
    return jnp.where(n < max_exact, n, jnp.minimum(large, N_BUCKETS - 1))


def diff_attention(q, q_pos, segments, rel_bias, lam, subln_w, lambda_init):
    bsz, tq = q.shape[:2]
    scale = ATTN_HD ** -0.5
    scores = []
    for k, _, k_pos in segments:
        rel = q_pos[:, None] - k_pos[None, :]
        tk = rel.shape[1]
        bias = rel_bias[t5_bucket(rel)].astype(jnp.float32)
        bias = jnp.transpose(bias.reshape(tq, tk, N_KV_HEADS, ATTN_GROUP), (2, 3, 0, 1))
        s = jnp.einsum('bqkgjd,bskjd->bkgjqs', q, k, preferred_element_type=jnp.float32) * scale
        s = s + bias[:, :, None]
        scores.append(jnp.where(rel >= 0, s, -jnp.inf))
    p = jax.nn.softmax(jnp.concatenate(scores, axis=-1), axis=-1)
    a = p[:, :, :, 0] - lam * p[:, :, :, 1]
    outs = []
    start = 0
    for k, v, _ in segments:
        tk = k.shape[1]
        outs.append(jnp.einsum('bkgqs,bskd->bqkgd', a[..., start:start + tk], v.astype(jnp.float32)))
        start += tk
    o = sum(outs)
    o = rms_norm(o, subln_w, 1e-5) * (1.0 - lambda_init)
    return o.reshape(bsz, tq, ATTN_OUT)


def diff_attention_blocks(q, pos0, segments, rel_bias, lam, subln_w, lambda_init):
    bsz, t = q.shape[:2]
    blk = min(t, ATTN_Q_BLOCK)
    n_blk = -(-t // blk)
    pad = n_blk * blk - t
    qb = jnp.pad(q, [(0, 0), (0, pad)] + [(0, 0)] * (q.ndim - 2))
    qb = jnp.moveaxis(qb.reshape((bsz, n_blk, blk) + q.shape[2:]), 1, 0)
    pb = (pos0 + jnp.arange(n_blk * blk)).reshape(n_blk, blk)
    ob = lax.map(lambda xs: diff_attention(xs[0], xs[1], segments, rel_bias, lam, subln_w, lambda_init), (qb, pb))
    return jnp.moveaxis(ob, 0, 1).reshape(bsz, n_blk * blk, ATTN_OUT)[:, :t]


def causal_conv(u, buf, w):
    t = u.shape[1]
    ext = jnp.concatenate([buf.astype(u.dtype), u], axis=1)
    y = sum(ext[:, j:j + t] * w[j] for j in range(GDN_CONV))
    return jax.nn.silu(y), ext[:, ext.shape[1] - (GDN_CONV - 1):]


def gated_delta_rule(q, k, v, g, beta, s0):
    bsz, t, nh, _ = q.shape
    c = GDN_CHUNK
    n = -(-t // c)
    pad = n * c - t

    def chunks(a):
        a = jnp.pad(a, [(0, 0), (0, pad)] + [(0, 0)] * (a.ndim - 2))
        a = a.reshape((bsz, n, c) + a.shape[2:])
        return jnp.moveaxis(a, (1, 3), (0, 2))

    qc, kc, vc, gc, bc = [chunks(a) for a in (q, k, v, g, beta)]
    gcum = jnp.cumsum(gc, axis=-1)
    causal = jnp.tril(jnp.ones((c, c), bool))
    strict = jnp.tril(jnp.ones((c, c), bool), -1)
    gdiff = gcum[..., :, None] - gcum[..., None, :]
    decay = jnp.where(causal, jnp.exp(jnp.where(causal, gdiff, 0.0)), 0.0)
    kb = kc * bc[..., None]
    a_mat = jnp.where(strict, jnp.einsum('nbhcd,nbhsd->nbhcs', kb, kc) * decay, 0.0)
    eye = jnp.eye(c, dtype=jnp.float32)
    t_mat = lax.linalg.triangular_solve(a_mat + eye, jnp.broadcast_to(eye, a_mat.shape), left_side=True, lower=True)
    u = jnp.einsum('nbhcs,nbhse->nbhce', t_mat, vc * bc[..., None])
    w = jnp.einsum('nbhcs,nbhsd->nbhcd', t_mat, kb * jnp.exp(gcum)[..., None])
    qk = jnp.where(causal, jnp.einsum('nbhcd,nbhsd->nbhcs', qc, kc) * decay, 0.0)

    def step(s, xs):
        q_i, k_i, u_i, w_i, qk_i, g_i = xs
        v_new = u_i - jnp.einsum('bhcd,bhde->bhce', w_i, s)
        o_i = (jnp.einsum('bhcd,bhde->bhce', q_i * jnp.exp(g_i)[..., None], s)
               + jnp.einsum('bhcs,bhse->bhce', qk_i, v_new))
        g_last = g_i[..., -1:]
        s = s * jnp.exp(g_last)[..., None] + jnp.einsum('bhcd,bhce->bhde', k_i * jnp.exp(g_last - g_i)[..., None], v_new)
        return s, o_i

    s_fin, o = lax.scan(step, s0, (qc, kc, u, w, qk, gcum))
    o = jnp.moveaxis(o, (0, 2), (1, 3)).reshape(bsz, n * c, nh, -1)[:, :t]
    return o, s_fin


def gdn_branch(qkv, z, b_in, a_in, conv_buf, s0, conv_w, a_log, dt_bias, norm_w):
    bsz, t = qkv.shape[:2]
    qkv, conv_new = causal_conv(qkv, conv_buf, conv_w)
    q, k, v = jnp.split(qkv.astype(jnp.float32), [GDN_HEADS * GDN_DK, 2 * GDN_HEADS * GDN_DK], axis=-1)
    q = l2_normalize(q.reshape(bsz, t, GDN_HEADS, GDN_DK)) * GDN_DK ** -0.5
    k = l2_normalize(k.reshape(bsz, t, GDN_HEADS, GDN_DK))
    v = v.reshape(bsz, t, GDN_HEADS, GDN_DV)
    beta = jax.nn.sigmoid(b_in.astype(jnp.float32))
    g = -jnp.exp(a_log.astype(jnp.float32)) * jax.nn.softplus(a_in.astype(jnp.float32) + dt_bias.astype(jnp.float32))
    o, s_new = gated_delta_rule(q, k, v, g, beta, s0.astype(jnp.float32))
    o = rms_norm(o, norm_w, 1e-6) * jax.nn.silu(z.astype(jnp.float32).reshape(bsz, t, GDN_HEADS, GDN_DV))
    return o.reshape(bsz, t, GDN_OUT), s_new.astype(s0.dtype), conv_new


def routed_experts(h, top_idx, top_w, w_gate, w_up, w_down):
    n_tok, d = h.shape
    n_rows = n_tok * TOP_K
    flat_e = top_idx.reshape(-1)
    flat_tok = jnp.repeat(jnp.arange(n_tok, dtype=jnp.int32), TOP_K)
    flat_w = top_w.reshape(-1)
    order = jnp.argsort(flat_e)
    e_sorted = flat_e[order]
    counts = jnp.bincount(flat_e, length=N_EXPERTS)
    padded = (counts + MOE_BLOCK - 1) // MOE_BLOCK * MOE_BLOCK
    pad_end = jnp.cumsum(padded)
    pad_start = pad_end - padded
    grp_start = jnp.cumsum(counts) - counts
    dest = pad_start[e_sorted] + jnp.arange(n_rows) - grp_start[e_sorted]
    n_blocks = -(-n_rows // MOE_BLOCK) + N_EXPERTS
    n_slots = n_blocks * MOE_BLOCK
    slot_tok = jnp.full((n_slots,), n_tok, jnp.int32).at[dest].set(flat_tok[order])
    slot_w = jnp.zeros((n_slots,), jnp.float32).at[dest].set(flat_w[order])
    block_e = jnp.minimum(jnp.searchsorted(pad_end, jnp.arange(n_blocks) * MOE_BLOCK, side='right'), N_EXPERTS - 1)
    h_pad = jnp.concatenate([h, jnp.zeros((1, d), h.dtype)], axis=0)

    def block(xs):
        tok, wt, e = xs
        xb = h_pad[tok]
        act = jax.nn.silu(xb @ w_gate[e]) * (xb @ w_up[e])
        return (act @ w_down[e]) * wt[:, None].astype(h.dtype)

    out = lax.map(block, (slot_tok.reshape(n_blocks, MOE_BLOCK), slot_w.reshape(n_blocks, MOE_BLOCK), block_e))
    return jax.ops.segment_sum(out.reshape(n_slots, d), slot_tok, num_segments=n_tok + 1)[:n_tok]


def moe_ffn(h, p):
    n_tok = h.shape[0]
    scores = jax.nn.sigmoid(jnp.einsum('td,de->te', h, p['router_w'], preferred_element_type=jnp.float32))
    choice = (scores + p['router_bias'].astype(jnp.float32)).reshape(n_tok, N_GROUPS, N_EXPERTS // N_GROUPS)
    grp_score = jnp.sum(lax.top_k(choice, 2)[0], axis=-1)
    top_grp = lax.top_k(grp_score, TOPK_GROUPS)[1]
    grp_keep = jnp.any(top_grp[:, :, None] == jnp.arange(N_GROUPS), axis=1)
    masked = jnp.where(grp_keep[:, :, None], choice, -jnp.inf).reshape(n_tok, N_EXPERTS)
    top_idx = lax.top_k(masked, TOP_K)[1]
    top_w = jnp.take_along_axis(scores, top_idx, axis=-1)
    top_w = top_w / jnp.sum(top_w, -1, keepdims=True) * ROUTED_SCALE
    routed = routed_experts(h, top_idx, top_w, p['exp_gate'], p['exp_up'], p['exp_down'])
    shared = (jax.nn.silu(h @ p['sh_gate']) * (h @ p['sh_up'])) @ p['sh_down']
    return routed + shared


def decoder_layer(x, c, past, conv_buf, s0, rel_bias, p, lambda_init):
    bsz, t, d = x.shape
    mod = jnp.einsum('bd,de->be', jax.nn.silu(c), p['ada_w']) + p['ada_b']
    sh1, sc1, g1, sh2, sc2, g2 = [m[:, None, :] for m in jnp.split(mod, 6, axis=-1)]
    h = x * (1 + sc1) + sh1
    q_a, k_a, v_a, qkv_g, z_g, b_g, a_g, gates = split_cols(h @ p['w_in'], IN_WIDTHS)

    q_a = q_a.reshape(bsz, t, N_KV_HEADS, ATTN_GROUP, 2, ATTN_HD)
    k_a = k_a.reshape(bsz, t, N_KV_HEADS, 2, ATTN_HD)
    v_a = v_a.reshape(bsz, t, N_KV_HEADS, ATTN_VD)
    if past is None:
        pos0 = 0
        segments = []
    else:
        pos0 = past[0].shape[1]
        segments = [(past[0], past[1], jnp.arange(pos0))]
    segments.append((k_a, v_a, pos0 + jnp.arange(t)))
    f32 = jnp.float32
    lam = (jnp.exp(jnp.sum(p['lam_q1'].astype(f32) * p['lam_k1'].astype(f32)))
           - jnp.exp(jnp.sum(p['lam_q2'].astype(f32) * p['lam_k2'].astype(f32))) + lambda_init)
    o_a = diff_attention_blocks(q_a, pos0, segments, rel_bias, lam, p['subln_w'], lambda_init)

    o_g, s_new, conv_new = gdn_branch(qkv_g, z_g, b_g, a_g, conv_buf, s0, p['conv_w'], p['a_log'],
                                      p['dt_bias'], p['gdn_norm_w'])

    gate_a, gate_g = jnp.split(jax.nn.sigmoid(gates.astype(f32)), 2, axis=-1)
    w_br = p['w_branch']
    merged = gate_a * (o_a.astype(x.dtype) @ w_br[:ATTN_OUT]) + gate_g * (o_g.astype(x.dtype) @ w_br[ATTN_OUT:])
    x = layer_norm(DEEPNORM_ALPHA * x + g1 * (merged.astype(x.dtype) @ p['w_o']), p['ln1_g'], p['ln1_b'])

    h2 = (x * (1 + sc2) + sh2).reshape(bsz * t, d)
    f = moe_ffn(h2, p).reshape(bsz, t, d)
    x = layer_norm(DEEPNORM_ALPHA * x + g2 * f, p['ln2_g'], p['ln2_b'])
    return x, k_a.reshape(bsz, t, N_KV_HEADS, 2 * ATTN_HD), v_a, s_new, conv_new


def setup_inputs(seed: int = 0) -> dict:
    key = jax.random.key(seed)
    ks = iter(jax.random.split(key, 48))
    f32 = jnp.float32
    L = DEPTH
    n_pages = PAST_LEN // PAGE_SIZE
    n_phys = (DEC_BATCH * n_pages * 5) // 4

    def nrm(shape, scale):
        return jax.random.normal(next(ks), shape, f32) * scale

    def gain(shape):
        return 1.0 + nrm(shape, 0.02)

    page_table = jax.random.permutation(next(ks), n_phys)[:DEC_BATCH * n_pages].reshape(DEC_BATCH, n_pages).astype(jnp.int32)
    dt = jnp.exp(jax.random.uniform(next(ks), (L, GDN_HEADS), f32, math.log(1e-3), math.log(1e-1)))
    a_log = jnp.log(jax.random.uniform(next(ks), (L, GDN_HEADS), f32, 1.0, 16.0))
    return {
        'x_prompt': nrm((BATCH, SEQ, D_MODEL), 1.0),
        'x_sample': nrm((DEC_BATCH, DEC_SEQ, D_MODEL), 1.0),
        'cache_k': nrm((L, n_phys, PAGE_SIZE, N_KV_HEADS, 2 * ATTN_HD), 1.0),
        'cache_v': nrm((L, n_phys, PAGE_SIZE, N_KV_HEADS, ATTN_VD), 1.0),
        'state_gdn': nrm((L, DEC_BATCH, GDN_HEADS, GDN_DK, GDN_DV), 0.5),
        'state_conv': nrm((L, DEC_BATCH, GDN_CONV - 1, GDN_QKV), 1.0),
        'page_table': page_table,
        'c_prompt': nrm((BATCH, D_MODEL), 1.0),
        'c_sample': nrm((DEC_BATCH, D_MODEL), 1.0),
        'rel_bias': nrm((N_BUCKETS, N_ATTN_HEADS), 0.5),
        'ada_w': nrm((L, D_MODEL, 6 * D_MODEL), 0.5 * D_MODEL ** -0.5),
        'ada_b': nrm((L, 6 * D_MODEL), 0.02),
        'w_in': nrm((L, D_MODEL, IN_WIDTH), D_MODEL ** -0.5),
        'lam_q1': nrm((L, ATTN_HD), 0.1),
        'lam_k1': nrm((L, ATTN_HD), 0.1),
        'lam_q2': nrm((L, ATTN_HD), 0.1),
        'lam_k2': nrm((L, ATTN_HD), 0.1),
        'subln_w': gain((L, ATTN_VD)),
        'conv_w': nrm((L, GDN_CONV, GDN_QKV), GDN_CONV ** -0.5),
        'a_log': a_log,
        'dt_bias': dt + jnp.log(-jnp.expm1(-dt)),
        'gdn_norm_w': gain((L, GDN_DV)),
        'w_branch': nrm((L, MIX_WIDTH, D_MODEL), ATTN_OUT ** -0.5),
        'w_o': nrm((L, D_MODEL, D_MODEL), DEEPNORM_BETA * D_MODEL ** -0.5),
        'ln1_g': gain((L, D_MODEL)),
        'ln1_b': nrm((L, D_MODEL), 0.02),
        'router_w': nrm((L, D_MODEL, N_EXPERTS), D_MODEL ** -0.5),
        'router_bias': nrm((L, N_EXPERTS), 0.01),
        'exp_gate': nrm((L, N_EXPERTS, D_MODEL, EXPERT_FF), D_MODEL ** -0.5),
        'exp_up': nrm((L, N_EXPERTS, D_MODEL, EXPERT_FF), D_MODEL ** -0.5),
        'exp_down': nrm((L, N_EXPERTS, EXPERT_FF, D_MODEL), DEEPNORM_BETA * EXPERT_FF ** -0.5),
        'sh_gate': nrm((L, D_MODEL, SHARED_FF), D_MODEL ** -0.5),
        'sh_up': nrm((L, D_MODEL, SHARED_FF), D_MODEL ** -0.5),
        'sh_down': nrm((L, SHARED_FF, D_MODEL), DEEPNORM_BETA * SHARED_FF ** -0.5),
        'ln2_g': gain((L, D_MODEL)),
        'ln2_b': nrm((L, D_MODEL), 0.02),
    }


def reference(x_prompt, x_sample, cache_k, cache_v, state_gdn, state_conv, page_table, c_prompt, c_sample,
              rel_bias, ada_w, ada_b, w_in, lam_q1, lam_k1, lam_q2, lam_k2, subln_w, conv_w, a_log, dt_bias,
              gdn_norm_w, w_branch, w_o, ln1_g, ln1_b, router_w, router_bias, exp_gate, exp_up, exp_down,
              sh_gate, sh_up, sh_down, ln2_g, ln2_b):
    bsz = x_prompt.shape[0]
    dec_b = x_sample.shape[0]
    past_len = page_table.shape[1] * cache_k.shape[2]
    y_prompt = x_prompt
    y_sample = x_sample
    kp_l, vp_l, sp_l, cp_l, ks_l, vs_l, ss_l, cs_l = [], [], [], [], [], [], [], []
    for l in range(DEPTH):
        lambda_init = 0.8 - 0.6 * math.exp(-0.3 * l)
        p = {'ada_w': ada_w[l], 'ada_b': ada_b[l], 'w_in': w_in[l],
             'lam_q1': lam_q1[l], 'lam_k1': lam_k1[l], 'lam_q2': lam_q2[l], 'lam_k2': lam_k2[l],
             'subln_w': subln_w[l], 'conv_w': conv_w[l], 'a_log': a_log[l], 'dt_bias': dt_bias[l],
             'gdn_norm_w': gdn_norm_w[l], 'w_branch': w_branch[l], 'w_o': w_o[l],
             'ln1_g': ln1_g[l], 'ln1_b': ln1_b[l], 'router_w': router_w[l], 'router_bias': router_bias[l],
             'exp_gate': exp_gate[l], 'exp_up': exp_up[l], 'exp_down': exp_down[l],
             'sh_gate': sh_gate[l], 'sh_up': sh_up[l], 'sh_down': sh_down[l],
             'ln2_g': ln2_g[l], 'ln2_b': ln2_b[l]}
        conv0 = jnp.zeros((bsz, GDN_CONV - 1, GDN_QKV), state_conv.dtype)
        s0 = jnp.zeros((bsz, GDN_HEADS, GDN_DK, GDN_DV), state_gdn.dtype)
        y_prompt, kp, vp, sp, cp = decoder_layer(y_prompt, c_prompt, None, conv0, s0, rel_bias, p, lambda_init)
        k_past = cache_k[l, page_table].reshape(dec_b, past_len, N_KV_HEADS, 2, ATTN_HD)
        v_past = cache_v[l, page_table].reshape(dec_b, past_len, N_KV_HEADS, ATTN_VD)
        y_sample, ks_, vs_, ss_, cs_ = decoder_layer(y_sample, c_sample, (k_past, v_past), state_conv[l],
                                                     state_gdn[l], rel_bias, p, lambda_init)
        kp_l.append(kp)
        vp_l.append(vp)
        sp_l.append(sp)
        cp_l.append(cp)
        ks_l.append(ks_)
        vs_l.append(vs_)
        ss_l.append(ss_)
        cs_l.append(cs_)
    k_prompt = jnp.stack(kp_l)
    v_prompt = jnp.stack(vp_l)
    gdn_prompt = jnp.stack(sp_l)
    conv_prompt = jnp.stack(cp_l)
    k_sample = jnp.stack(ks_l)
    v_sample = jnp.stack(vs_l)
    gdn_sample = jnp.stack(ss_l)
    conv_sample = jnp.stack(cs_l)
    return (y_prompt, y_sample, k_prompt, v_prompt, gdn_prompt, conv_prompt, k_sample, v_sample, gdn_sample, conv_sample)
```

```python
import functools
import math

import jax
import jax.numpy as jnp
from jax import lax
from jax.experimental import pallas as pl
from jax.experimental.pallas import tpu as pltpu

F32 = jnp.float32
BF16 = jnp.bfloat16
U32 = jnp.uint32
I32 = jnp.int32

N_ATTN_HEADS = 16
N_KV_HEADS = 4
ATTN_GROUP = N_ATTN_HEADS // N_KV_HEADS
ATTN_HD = 128
ATTN_VD = 256
N_BUCKETS = 32
MAX_DISTANCE = 128
GDN_HEADS = 32
GDN_DK = 128
GDN_DV = 128
GDN_CONV = 4
GDN_CHUNK = 64
N_EXPERTS = 128
TOP_K = 8
N_GROUPS = 8
GROUP_SIZE = N_EXPERTS // N_GROUPS
TOPK_GROUPS = 4
ROUTED_SCALE = 2.5

LANES = 128
SUBLANES = 8
VMEM_LIMIT = 56 * 1024 * 1024

ROW_TILE = 128
ATTN_BLOCK = 256
PAGES_PER_STEP = 4
GDN_HEAD_GROUP = 8
MOE_ROW_BLOCK = 256
GATHER_ROWS = 256

NEG_INF = float("-inf")


def _cparams(sem, vmem=VMEM_LIMIT):
    return pltpu.CompilerParams(dimension_semantics=sem, vmem_limit_bytes=vmem)


def _sigmoid(x):
    return jax.nn.sigmoid(x)


def _dot(a, b):
    return jnp.dot(a, b, preferred_element_type=F32)


def _dot_nt(a, b):
    return lax.dot_general(a, b, (((1,), (1,)), ((), ())), preferred_element_type=F32)


def _dot_tn(a, b):
    return lax.dot_general(a, b, (((0,), (0,)), ((), ())), preferred_element_type=F32)


def _mm_body(*refs, a_silu, has_bias, has_gate, has_prev):
    it = iter(refs)
    a_ref = next(it)
    w_ref = next(it)
    bias_ref = next(it) if has_bias else None
    gate_ref = next(it) if has_gate else None
    prev_ref = next(it) if has_prev else None
    o_ref = next(it)
    a = a_ref[...]
    if a_silu:
        a = a.astype(F32)
        a = a * _sigmoid(a)
    acc = _dot(a.astype(BF16), w_ref[...].astype(BF16))
    if has_bias:
        acc = acc + bias_ref[...]
    if has_gate:
        acc = _sigmoid(gate_ref[...]) * acc
    if has_prev:
        acc = acc + prev_ref[...]
    o_ref[...] = acc.astype(o_ref.dtype)


def _mm(a, w, *, tm, tn, n_out, w_row_blk=0, w_col_blk=0, bias=None, gate=None, gate_col_blk=0,
        prev=None, a_silu=False, out_dtype=F32):
    m, k = a.shape
    assert m % tm == 0 and n_out % tn == 0
    in_specs = [pl.BlockSpec((tm, k), lambda i, j: (i, 0)),
                pl.BlockSpec((k, tn), lambda i, j: (w_row_blk, j + w_col_blk))]
    args = [a, w]
    if bias is not None:
        in_specs.append(pl.BlockSpec((1, tn), lambda i, j: (0, j)))
        args.append(bias)
    if gate is not None:
        in_specs.append(pl.BlockSpec((tm, tn), lambda i, j: (i, j + gate_col_blk)))
        args.append(gate)
    if prev is not None:
        in_specs.append(pl.BlockSpec((tm, tn), lambda i, j: (i, j)))
        args.append(prev)
    body = functools.partial(_mm_body, a_silu=a_silu, has_bias=bias is not None,
                             has_gate=gate is not None, has_prev=prev is not None)
    return pl.pallas_call(
        body,
        grid=(m // tm, n_out // tn),
        in_specs=in_specs,
        out_specs=pl.BlockSpec((tm, tn), lambda i, j: (i, j)),
        out_shape=jax.ShapeDtypeStruct((m, n_out), out_dtype),
        compiler_params=_cparams(("parallel", "arbitrary")),
    )(*args)


def _pick_tm(m, cap):
    best = None
    for t in range(LANES, cap + 1, LANES):
        if m % t == 0:
            best = t
    assert best is not None
    return best


def _row_mod(i, n_prompt_tiles, p_ref, s_ref):
    return jnp.where(i >= n_prompt_tiles, s_ref[...], p_ref[0])


def _mod_specs(d, chunk, tiles_per_seq, n_prompt_tiles):
    p_spec = pl.BlockSpec((1, 1, d), lambda i: (jnp.minimum(i, n_prompt_tiles - 1) // tiles_per_seq, 0, chunk))
    s_spec = pl.BlockSpec((ROW_TILE, d), lambda i: (jnp.maximum(i - n_prompt_tiles, 0), chunk))
    return p_spec, s_spec


def _modulate_body(x_ref, scp_ref, scs_ref, shp_ref, shs_ref, o_ref, *, n_prompt_tiles):
    i = pl.program_id(0)
    sc = _row_mod(i, n_prompt_tiles, scp_ref, scs_ref)
    sh = _row_mod(i, n_prompt_tiles, shp_ref, shs_ref)
    o_ref[...] = (x_ref[...] * (1.0 + sc) + sh).astype(o_ref.dtype)


def _modulate(x, mod_p, mod_s, sc_chunk, sh_chunk, seq):
    t, d = x.shape
    n_prompt_tiles = mod_p.shape[0] * seq // ROW_TILE
    tps = seq // ROW_TILE
    scp, scs = _mod_specs(d, sc_chunk, tps, n_prompt_tiles)
    shp, shs = _mod_specs(d, sh_chunk, tps, n_prompt_tiles)
    return pl.pallas_call(
        functools.partial(_modulate_body, n_prompt_tiles=n_prompt_tiles),
        grid=(t // ROW_TILE,),
        in_specs=[pl.BlockSpec((ROW_TILE, d), lambda i: (i, 0)), scp, scs, shp, shs],
        out_specs=pl.BlockSpec((ROW_TILE, d), lambda i: (i, 0)),
        out_shape=jax.ShapeDtypeStruct((t, d), BF16),
        compiler_params=_cparams(("parallel",)),
    )(x, mod_p, mod_s, mod_p, mod_s)


def _layer_norm_rows(v, g, b):
    mu = jnp.mean(v, -1, keepdims=True)
    var = jnp.mean(jnp.square(v - mu), -1, keepdims=True)
    return (v - mu) * lax.rsqrt(var + 1e-5) * g + b


def _pack_bf16_pairs(h):
    half = h.shape[1] // 2
    bits = lax.bitcast_convert_type(h.astype(BF16).astype(F32), U32)
    return (bits[:, :half] >> 16) | (bits[:, half:] & jnp.uint32(0xFFFF0000))


def _unpack_bf16_pairs(p):
    lo = lax.bitcast_convert_type(p << 16, F32).astype(BF16)
    hi = lax.bitcast_convert_type(p & jnp.uint32(0xFFFF0000), F32).astype(BF16)
    return lo, hi


def _ln1_body(x_ref, y_ref, gp_ref, gs_ref, scp_ref, scs_ref, shp_ref, shs_ref, lg_ref, lb_ref,
              x1_ref, hp_ref, *, n_prompt_tiles, alpha):
    i = pl.program_id(0)
    g1 = _row_mod(i, n_prompt_tiles, gp_ref, gs_ref)
    sc = _row_mod(i, n_prompt_tiles, scp_ref, scs_ref)
    sh = _row_mod(i, n_prompt_tiles, shp_ref, shs_ref)
    x1 = _layer_norm_rows(alpha * x_ref[...] + g1 * y_ref[...], lg_ref[...], lb_ref[...])
    x1_ref[...] = x1
    hp_ref[...] = _pack_bf16_pairs(x1 * (1.0 + sc) + sh)


def _ln1(x, y, mod_p, mod_s, ln_g, ln_b, seq, alpha):
    t, d = x.shape
    n_prompt_tiles = mod_p.shape[0] * seq // ROW_TILE
    tps = seq // ROW_TILE
    gp, gs = _mod_specs(d, 2, tps, n_prompt_tiles)
    shp, shs = _mod_specs(d, 3, tps, n_prompt_tiles)
    scp, scs = _mod_specs(d, 4, tps, n_prompt_tiles)
    row = pl.BlockSpec((ROW_TILE, d), lambda i: (i, 0))
    vec = pl.BlockSpec((1, d), lambda i: (0, 0))
    return pl.pallas_call(
        functools.partial(_ln1_body, n_prompt_tiles=n_prompt_tiles, alpha=alpha),
        grid=(t // ROW_TILE,),
        in_specs=[row, row, gp, gs, scp, scs, shp, shs, vec, vec],
        out_specs=[row, pl.BlockSpec((ROW_TILE, d // 2), lambda i: (i, 0))],
        out_shape=[jax.ShapeDtypeStruct((t, d), F32), jax.ShapeDtypeStruct((t, d // 2), U32)],
        compiler_params=_cparams(("parallel",)),
    )(x, y, mod_p, mod_s, mod_p, mod_s, mod_p, mod_s, ln_g, ln_b)


def _final_body(x_ref, yt_ref, tw_ref, sh_ref, gp_ref, gs_ref, lg_ref, lb_ref, o_ref, *,
                n_prompt_tiles, alpha, d):
    i = pl.program_id(0)
    g2 = _row_mod(i, n_prompt_tiles, gp_ref, gs_ref)
    f = sh_ref[...]
    tw = tw_ref[...]
    for k in range(TOP_K):
        f = f + yt_ref[:, k * d:(k + 1) * d] * tw[:, k:k + 1]
    o_ref[...] = _layer_norm_rows(alpha * x_ref[...] + g2 * f, lg_ref[...], lb_ref[...])


def _final(x1, y_tok, top_w, shared, mod_p, mod_s, ln_g, ln_b, seq, alpha):
    t, d = x1.shape
    n_prompt_tiles = mod_p.shape[0] * seq // ROW_TILE
    gp, gs = _mod_specs(d, 5, seq // ROW_TILE, n_prompt_tiles)
    row = pl.BlockSpec((ROW_TILE, d), lambda i: (i, 0))
    vec = pl.BlockSpec((1, d), lambda i: (0, 0))
    return pl.pallas_call(
        functools.partial(_final_body, n_prompt_tiles=n_prompt_tiles, alpha=alpha, d=d),
        grid=(t // ROW_TILE,),
        in_specs=[row, pl.BlockSpec((ROW_TILE, TOP_K * d), lambda i: (i, 0)),
                  pl.BlockSpec((ROW_TILE, LANES), lambda i: (i, 0)), row, gp, gs, vec, vec],
        out_specs=row,
        out_shape=jax.ShapeDtypeStruct((t, d), F32),
        compiler_params=_cparams(("parallel",)),
    )(x1, y_tok, top_w, shared, mod_p, mod_s, ln_g, ln_b)


def _t5_bucket(rel):
    n = jnp.maximum(rel, 0)
    max_exact = N_BUCKETS // 2
    nf = jnp.maximum(n, 1).astype(F32)
    large = max_exact + (jnp.log(nf / max_exact) / math.log(MAX_DISTANCE / max_exact)
                         * (N_BUCKETS - max_exact)).astype(I32)
    return jnp.where(n < max_exact, n, jnp.minimum(large, N_BUCKETS - 1))


def _lambda(lam_ref, lam_init):
    lam = lam_ref[...]
    s1 = jnp.sum(lam[0:1] * lam[1:2], axis=-1, keepdims=True)
    s2 = jnp.sum(lam[2:3] * lam[3:4], axis=-1, keepdims=True)
    return jnp.exp(s1) - jnp.exp(s2) + lam_init


def _sub_norm(o, w, lam_init):
    return o * lax.rsqrt(jnp.mean(o * o, -1, keepdims=True) + 1e-5) * w * (1.0 - lam_init)


def _online_update(idx, s, v, m_s, l_s, acc_s):
    m_old = m_s[idx]
    m_new = jnp.maximum(m_old, jnp.max(s, axis=-1, keepdims=True))
    p = jnp.exp(s - m_new)
    corr = jnp.exp(m_old - m_new)
    l_s[idx] = corr * l_s[idx] + jnp.sum(p, axis=-1, keepdims=True)
    acc_s[idx] = corr * acc_s[idx] + _dot(p.astype(BF16), v)
    m_s[idx] = m_new


def _flash_body(far_ref, q_ref, k_ref, v_ref, bt_ref, lam_ref, sub_ref, o_ref, m_s, l_s, acc_s, *,
                scale, lam_init):
    kv = pl.program_id(1)
    qi = pl.program_id(2)
    ki = pl.program_id(3)
    hd, vd, grp = ATTN_HD, ATTN_VD, ATTN_GROUP

    @pl.when(ki == 0)
    def _():
        m_s[...] = jnp.full(m_s.shape, NEG_INF, F32)
        l_s[...] = jnp.zeros(l_s.shape, F32)
        acc_s[...] = jnp.zeros(acc_s.shape, F32)

    def process(get_bias):
        k = k_ref[...].astype(BF16)
        v = v_ref[...].astype(BF16)
        for g in range(grp):
            bias = get_bias(g)
            for j in range(2):
                c0 = g * 2 * hd + j * hd
                q = q_ref[:, c0:c0 + hd].astype(BF16)
                s = _dot_nt(q, k[:, j * hd:(j + 1) * hd]) * scale + bias
                _online_update(g * 2 + j, s, v, m_s, l_s, acc_s)

    @pl.when(ki < qi - 1)
    def _():
        process(lambda g: far_ref[kv * grp + g])

    @pl.when(ki == qi - 1)
    def _():
        process(lambda g: bt_ref[g, 1])

    @pl.when(ki == qi)
    def _():
        process(lambda g: bt_ref[g, 0])
        lam = _lambda(lam_ref, lam_init)
        for g in range(grp):
            o1 = acc_s[g * 2] / l_s[g * 2]
            o2 = acc_s[g * 2 + 1] / l_s[g * 2 + 1]
            o = _sub_norm(o1 - lam * o2, sub_ref[...], lam_init)
            o_ref[:, g * vd:(g + 1) * vd] = o.astype(o_ref.dtype)


def _flash_attention(proj, bias_tiles, bias_far, lam_vecs, subln_w, bsz, seq, lam_init):
    blk = ATTN_BLOCK
    nblk = seq // blk
    grp, hd, vd = ATTN_GROUP, ATTN_HD, ATTN_VD
    qw = grp * 2 * hd
    k_col0 = N_KV_HEADS * qw // (2 * hd)
    v_col0 = k_col0 + N_KV_HEADS
    grid_spec = pltpu.PrefetchScalarGridSpec(
        num_scalar_prefetch=0,
        grid=(bsz, N_KV_HEADS, nblk, nblk),
        in_specs=[
            pl.BlockSpec(memory_space=pltpu.SMEM),
            pl.BlockSpec((blk, qw), lambda b, kv, qi, ki: (b * nblk + qi, kv)),
            pl.BlockSpec((blk, 2 * hd), lambda b, kv, qi, ki: (b * nblk + jnp.minimum(ki, qi), k_col0 + kv)),
            pl.BlockSpec((blk, vd), lambda b, kv, qi, ki: (b * nblk + jnp.minimum(ki, qi), v_col0 + kv)),
            pl.BlockSpec((grp, 2, blk, blk), lambda b, kv, qi, ki: (kv, 0, 0, 0)),
            pl.BlockSpec((4, hd), lambda b, kv, qi, ki: (0, 0)),
            pl.BlockSpec((1, vd), lambda b, kv, qi, ki: (0, 0)),
        ],
        out_specs=pl.BlockSpec((blk, grp * vd), lambda b, kv, qi, ki: (b * nblk + qi, kv)),
        scratch_shapes=[pltpu.VMEM((2 * grp, blk, 1), F32), pltpu.VMEM((2 * grp, blk, 1), F32),
                        pltpu.VMEM((2 * grp, blk, vd), F32)],
    )
    return pl.pallas_call(
        functools.partial(_flash_body, scale=hd ** -0.5, lam_init=lam_init),
        grid_spec=grid_spec,
        out_shape=jax.ShapeDtypeStruct((bsz * seq, N_ATTN_HEADS * vd), BF16),
        compiler_params=_cparams(("parallel", "parallel", "parallel", "arbitrary")),
    )(bias_far, proj, proj, proj, bias_tiles, lam_vecs, subln_w)


def _paged_body(pt_ref, qbd_ref, kn_ref, vn_ref, sb_ref, sbs_ref, lam_ref, sub_ref, *rest, npp, scale,
                lam_init):
    k_refs = rest[:npp]
    v_refs = rest[npp:2 * npp]
    o_ref = rest[2 * npp]
    m_s, l_s, acc_s = rest[2 * npp + 1:]
    pg = pl.program_id(1)
    grp, vd = ATTN_GROUP, ATTN_VD
    kw = 2 * ATTN_HD

    @pl.when(pg == 0)
    def _():
        for kv in range(N_KV_HEADS):
            q = qbd_ref[0, kv].astype(BF16).astype(F32)
            kn = kn_ref[0, kv].astype(BF16).astype(F32)
            s = jnp.sum(q * kn, axis=-1, keepdims=True) * scale + sbs_ref[kv][:, 0:1]
            m_s[kv] = s
            l_s[kv] = jnp.ones(s.shape, F32)
            acc_s[kv] = jnp.broadcast_to(vn_ref[0, kv].astype(BF16).astype(F32), (2 * grp, vd))

    for i in range(npp):
        for kv in range(N_KV_HEADS):
            q = qbd_ref[0, kv].astype(BF16)
            k = k_refs[i][:, kv * kw:(kv + 1) * kw].astype(BF16)
            v = v_refs[i][:, kv * vd:(kv + 1) * vd].astype(BF16)
            s = _dot_nt(q, k) * scale + sb_ref[pg * npp + i, kv]
            _online_update(kv, s, v, m_s, l_s, acc_s)

    @pl.when(pg == pl.num_programs(1) - 1)
    def _():
        lam = _lambda(lam_ref, lam_init)
        for kv in range(N_KV_HEADS):
            n = acc_s[kv] / l_s[kv]
            o = _sub_norm(n[0:grp] - lam * n[grp:2 * grp], sub_ref[...], lam_init)
            for g in range(grp):
                c0 = (kv * grp + g) * vd
                o_ref[0, :, c0:c0 + vd] = o[g:g + 1]


def _paged_attention(page_table, qbd, k_new, v_new, cache_k, cache_v, layer, sbias, sbias_self, lam_vecs,
                     subln_w, lam_init):
    bs, n_pages = page_table.shape
    npp = math.gcd(PAGES_PER_STEP, n_pages)
    page = cache_k.shape[2]
    kw = N_KV_HEADS * 2 * ATTN_HD
    vw = N_KV_HEADS * ATTN_VD
    grp, vd = ATTN_GROUP, ATTN_VD

    def page_spec(width, i):
        return pl.BlockSpec((None, None, page, width),
                            lambda b, pg, pt: (layer, pt[b, pg * npp + i], 0, 0))

    grid_spec = pltpu.PrefetchScalarGridSpec(
        num_scalar_prefetch=1,
        grid=(bs, n_pages // npp),
        in_specs=[
            pl.BlockSpec((1, N_KV_HEADS, 2 * grp, 2 * ATTN_HD), lambda b, pg, pt: (b, 0, 0, 0)),
            pl.BlockSpec((1, N_KV_HEADS, 1, 2 * ATTN_HD), lambda b, pg, pt: (b, 0, 0, 0)),
            pl.BlockSpec((1, N_KV_HEADS, 1, vd), lambda b, pg, pt: (b, 0, 0, 0)),
            pl.BlockSpec((n_pages, N_KV_HEADS, 2 * grp, page), lambda b, pg, pt: (0, 0, 0, 0)),
            pl.BlockSpec((N_KV_HEADS, 2 * grp, LANES), lambda b, pg, pt: (0, 0, 0)),
            pl.BlockSpec((4, ATTN_HD), lambda b, pg, pt: (0, 0)),
            pl.BlockSpec((1, vd), lambda b, pg, pt: (0, 0)),
        ] + [page_spec(kw, i) for i in range(npp)] + [page_spec(vw, i) for i in range(npp)],
        out_specs=pl.BlockSpec((1, 1, N_ATTN_HEADS * vd), lambda b, pg, pt: (b, 0, 0)),
        scratch_shapes=[pltpu.VMEM((N_KV_HEADS, 2 * grp, 1), F32), pltpu.VMEM((N_KV_HEADS, 2 * grp, 1), F32),
                        pltpu.VMEM((N_KV_HEADS, 2 * grp, vd), F32)],
    )
    ck = cache_k.reshape(cache_k.shape[0], cache_k.shape[1], page, kw)
    cv = cache_v.reshape(cache_v.shape[0], cache_v.shape[1], page, vw)
    return pl.pallas_call(
        functools.partial(_paged_body, npp=npp, scale=ATTN_HD ** -0.5, lam_init=lam_init),
        grid_spec=grid_spec,
        out_shape=jax.ShapeDtypeStruct((bs, 1, N_ATTN_HEADS * vd), F32),
        compiler_params=_cparams(("parallel", "arbitrary")),
    )(page_table, qbd, k_new, v_new, sbias, sbias_self, lam_vecs, subln_w, *([ck] * npp), *([cv] * npp))


def _softplus(x):
    return jnp.maximum(x, 0.0) + jnp.log(1.0 + jnp.exp(-jnp.abs(x)))


def _l2norm(x):
    return x * lax.rsqrt(jnp.sum(x * x, -1, keepdims=True) + 1e-6)


def _unit_lower_inverse(a):
    n = a.shape[0]
    eye = (lax.broadcasted_iota(I32, (n, n), 0) == lax.broadcasted_iota(I32, (n, n), 1)).astype(F32)
    hp = functools.partial(jnp.dot, preferred_element_type=F32, precision=lax.Precision.HIGHEST)
    y = -a
    r = eye + y
    span = 2
    while span < n:
        y = hp(y, y)
        r = r + hp(r, y)
        span *= 2
    return r


def _gdn_body(q_ref, k_ref, v_ref, qp_ref, kp_ref, vp_ref, z_ref, ba_ref, cwq_ref, cwk_ref, cwv_ref,
              alog_ref, dtb_ref, nw_ref, o_ref, sout_ref, s_s, *, hg_size, chunk):
    hg = pl.program_id(1)
    c = pl.program_id(2)
    first = c == 0
    dk, dv = GDN_DK, GDN_DV

    @pl.when(first)
    def _():
        s_s[...] = jnp.zeros(s_s.shape, F32)

    def conv_silu(cur_ref, prev_ref, w_ref):
        cur = cur_ref[...]
        prev = jnp.where(first, 0.0, prev_ref[...])
        ext = jnp.concatenate([prev, cur], axis=0)
        y = cur * w_ref[GDN_CONV - 1:GDN_CONV, :]
        for d in range(1, GDN_CONV):
            shifted = pltpu.roll(ext, d, 0)[SUBLANES:SUBLANES + chunk]
            y = y + shifted * w_ref[GDN_CONV - 1 - d:GDN_CONV - d, :]
        return y * _sigmoid(y)

    qc = conv_silu(q_ref, qp_ref, cwq_ref)
    kc = conv_silu(k_ref, kp_ref, cwk_ref)
    vc = conv_silu(v_ref, vp_ref, cwv_ref)

    raw = ba_ref[...]
    off = hg * hg_size
    b_raw = pltpu.roll(raw, (LANES - off) % LANES, 1)
    a_raw = pltpu.roll(raw, (2 * LANES - GDN_HEADS - off) % LANES, 1)
    beta_all = _sigmoid(b_raw)
    g_all = -jnp.exp(alog_ref[0]) * _softplus(a_raw + dtb_ref[0])
    rows = lax.broadcasted_iota(I32, g_all.shape, 0)
    gc_all = g_all
    span = 1
    while span < chunk:
        gc_all = gc_all + jnp.where(rows >= span, pltpu.roll(gc_all, span, 0), 0.0)
        span *= 2
    gc_t = gc_all.T
    eg_all = jnp.exp(gc_all)

    ri = lax.broadcasted_iota(I32, (chunk, chunk), 0)
    ci = lax.broadcasted_iota(I32, (chunk, chunk), 1)
    causal = ri >= ci
    strict = ri > ci

    for h in range(hg_size):
        qh = _l2norm(qc[:, h * dk:(h + 1) * dk]) * dk ** -0.5
        kh = _l2norm(kc[:, h * dk:(h + 1) * dk])
        vh = vc[:, h * dv:(h + 1) * dv]
        beta = beta_all[:, h:h + 1]
        gcol = gc_all[:, h:h + 1]
        grow = gc_t[h:h + 1, :]
        egc = eg_all[:, h:h + 1]
        decay = jnp.where(causal, jnp.exp(jnp.where(causal, gcol - grow, 0.0)), 0.0)
        kb = kh * beta
        kbf = kh.astype(BF16)
        a_mat = jnp.where(strict, _dot_nt(kb.astype(BF16), kbf) * decay, 0.0)
        t_mat = _unit_lower_inverse(a_mat).astype(BF16)
        u = _dot(t_mat, (vh * beta).astype(BF16))
        w = _dot(t_mat, (kb * egc).astype(BF16))
        qk = jnp.where(causal, _dot_nt(qh.astype(BF16), kbf) * decay, 0.0)
        s_old = s_s[h]
        s_bf = s_old.astype(BF16)
        v_new = u - _dot(w.astype(BF16), s_bf)
        v_new_bf = v_new.astype(BF16)
        o = _dot((qh * egc).astype(BF16), s_bf) + _dot(qk.astype(BF16), v_new_bf)
        g_last = gcol[chunk - 1:chunk, :]
        s_s[h] = s_old * jnp.exp(g_last) + _dot_tn((kh * jnp.exp(g_last - gcol)).astype(BF16), v_new_bf)
        zh = z_ref[:, h * dv:(h + 1) * dv]
        on = o * lax.rsqrt(jnp.mean(o * o, -1, keepdims=True) + 1e-6) * nw_ref[...] * (zh * _sigmoid(zh))
        o_ref[:, h * dv:(h + 1) * dv] = on.astype(o_ref.dtype)

    @pl.when(c == pl.num_programs(2) - 1)
    def _():
        sout_ref[0] = s_s[...]


def _gdn_prompt(proj, ba, conv_w, alog_g, dtb_g, norm_w, bsz, seq, col0):
    hgs, chunk = GDN_HEAD_GROUP, GDN_CHUNK
    n_hg = GDN_HEADS // hgs
    w = hgs * GDN_DK
    n_chunks = seq // chunk
    hw = GDN_HEADS * GDN_DK
    assert col0 % w == 0 and seq % chunk == 0
    qb, kb, vb, zb = (col0 // w + i * (hw // w) for i in range(4))
    cpt = chunk // SUBLANES

    def cur(base):
        return pl.BlockSpec((chunk, w), lambda b, hg, c: (b * n_chunks + c, base + hg))

    def prev(base):
        return pl.BlockSpec((SUBLANES, w),
                            lambda b, hg, c: (jnp.maximum((b * n_chunks + c) * cpt - 1, 0), base + hg))

    def cw(base):
        return pl.BlockSpec((GDN_CONV, w), lambda b, hg, c: (0, base + hg))

    return pl.pallas_call(
        functools.partial(_gdn_body, hg_size=hgs, chunk=chunk),
        grid=(bsz, n_hg, n_chunks),
        in_specs=[cur(qb), cur(kb), cur(vb), prev(qb), prev(kb), prev(vb), cur(zb),
                  pl.BlockSpec((chunk, LANES), lambda b, hg, c: (b * n_chunks + c, 0)),
                  cw(0), cw(hw // w), cw(2 * hw // w),
                  pl.BlockSpec((1, 1, LANES), lambda b, hg, c: (hg, 0, 0)),
                  pl.BlockSpec((1, 1, LANES), lambda b, hg, c: (hg, 0, 0)),
                  pl.BlockSpec((1, GDN_DV), lambda b, hg, c: (0, 0))],
        out_specs=[pl.BlockSpec((chunk, w), lambda b, hg, c: (b * n_chunks + c, hg)),
                   pl.BlockSpec((1, hgs, GDN_DK, GDN_DV), lambda b, hg, c: (b, hg, 0, 0))],
        out_shape=[jax.ShapeDtypeStruct((bsz * seq, hw), BF16),
                   jax.ShapeDtypeStruct((bsz, GDN_HEADS, GDN_DK, GDN_DV), F32)],
        scratch_shapes=[pltpu.VMEM((hgs, GDN_DK, GDN_DV), F32)],
        compiler_params=_cparams(("parallel", "parallel", "arbitrary")),
    )(proj, proj, proj, proj, proj, proj, proj, ba, conv_w, conv_w, conv_w, alog_g, dtb_g, norm_w)


def _gdn_step_body(new_ref, buf_ref, cw_ref, z_ref, b_ref, a_ref, alog_ref, dtb_ref, nw_ref, s_ref,
                   o_ref, sout_ref, cout_ref):
    nh = GDN_HEADS
    new = new_ref[0]
    y = new * cw_ref[GDN_CONV - 1]
    for j in range(GDN_CONV - 1):
        y = y + buf_ref[0, j] * cw_ref[j]
        if j > 0:
            cout_ref[0, j - 1] = buf_ref[0, j]
    cout_ref[0, GDN_CONV - 2] = new
    y = y * _sigmoid(y)
    q = _l2norm(y[0:nh]) * GDN_DK ** -0.5
    k = _l2norm(y[nh:2 * nh])
    v = y[2 * nh:3 * nh]
    beta = _sigmoid(b_ref[0])
    eg = jnp.exp(-jnp.exp(alog_ref[...]) * _softplus(a_ref[0] + dtb_ref[...]))
    q_t = q.T
    k_t = k.T
    z = z_ref[0]
    for h in range(nh):
        s_old = s_ref[0, h]
        kcol = k_t[:, h:h + 1]
        egh = eg[h:h + 1, :]
        sk = jnp.sum(kcol * s_old, axis=0, keepdims=True)
        v_new = beta[h:h + 1, :] * (v[h:h + 1, :] - egh * sk)
        s_new = s_old * egh + kcol * v_new
        sout_ref[0, h] = s_new
        o = jnp.sum(q_t[:, h:h + 1] * s_new, axis=0, keepdims=True)
        zh = z[h:h + 1, :]
        on = o * lax.rsqrt(jnp.mean(o * o, -1, keepdims=True) + 1e-6) * nw_ref[...] * (zh * _sigmoid(zh))
        o_ref[0, h:h + 1, :] = on


def _gdn_step(qkv_new, conv_buf, conv_w, z, b_in, a_in, a_log, dt_bias, norm_w, state):
    bs = qkv_new.shape[0]
    nh, nr = GDN_HEADS, 3 * GDN_HEADS
    per_b3 = lambda shape: pl.BlockSpec((1,) + shape, lambda b: (b, 0, 0))
    per_b4 = lambda shape: pl.BlockSpec((1,) + shape, lambda b: (b, 0, 0, 0))
    return pl.pallas_call(
        _gdn_step_body,
        grid=(bs,),
        in_specs=[per_b3((nr, GDN_DK)), per_b4((GDN_CONV - 1, nr, GDN_DK)),
                  pl.BlockSpec((GDN_CONV, nr, GDN_DK), lambda b: (0, 0, 0)),
                  per_b3((nh, GDN_DV)), per_b3((nh, 1)), per_b3((nh, 1)),
                  pl.BlockSpec((nh, 1), lambda b: (0, 0)), pl.BlockSpec((nh, 1), lambda b: (0, 0)),
                  pl.BlockSpec((1, GDN_DV), lambda b: (0, 0)),
                  per_b4((nh, GDN_DK, GDN_DV))],
        out_specs=[per_b3((nh, GDN_DV)), per_b4((nh, GDN_DK, GDN_DV)), per_b4((GDN_CONV - 1, nr, GDN_DK))],
        out_shape=[jax.ShapeDtypeStruct((bs, nh, GDN_DV), F32),
                   jax.ShapeDtypeStruct((bs, nh, GDN_DK, GDN_DV), F32),
                   jax.ShapeDtypeStruct((bs, GDN_CONV - 1, nr, GDN_DK), F32)],
        compiler_params=_cparams(("parallel",)),
    )(qkv_new, conv_buf, conv_w, z, b_in, a_in, a_log, dt_bias, norm_w, state)


def _lane_partner(x, lane, s):
    return jnp.where((lane & s) != 0, pltpu.roll(x, s, 1), pltpu.roll(x, LANES - s, 1))


def _group_reduce(x, lane, op):
    s = 1
    while s < GROUP_SIZE:
        x = op(x, _lane_partner(x, lane, s))
        s *= 2
    return x


def _router_body(hp_ref, rw_ref, rb_ref, idx_ref, w_ref, rank_ref, cnt_ref, carry_s):
    i = pl.program_id(0)

    @pl.when(i == 0)
    def _():
        carry_s[...] = jnp.zeros(carry_s.shape, F32)

    lo, hi = _unpack_bf16_pairs(hp_ref[...])
    half = lo.shape[1]
    rw = rw_ref[...].astype(BF16)
    scores = _sigmoid(_dot(lo, rw[:half]) + _dot(hi, rw[half:]))
    choice = scores + rb_ref[...]
    tm = scores.shape[0]
    lane = lax.broadcasted_iota(I32, (tm, LANES), 1)
    grp = lane // GROUP_SIZE
    big = jnp.int32(2 * LANES)

    m1 = _group_reduce(choice, lane, jnp.maximum)
    first = _group_reduce(jnp.where(choice == m1, lane, big), lane, jnp.minimum)
    m2 = _group_reduce(jnp.where(lane == first, NEG_INF, choice), lane, jnp.maximum)
    gs = m1 + m2
    beaten = jnp.zeros((tm, LANES), I32)
    for d in range(1, N_GROUPS):
        other = pltpu.roll(gs, d * GROUP_SIZE, 1)
        other_grp = pltpu.roll(grp, d * GROUP_SIZE, 1)
        beats = (other > gs) | ((other == gs) & (other_grp < grp))
        beaten = beaten + beats.astype(I32)
    masked = jnp.where(beaten < TOPK_GROUPS, choice, NEG_INF)

    idx_out = jnp.zeros((tm, LANES), I32)
    w_out = jnp.zeros((tm, LANES), F32)
    onehot = jnp.zeros((tm, LANES), F32)
    sels = []
    for k in range(TOP_K):
        m = jnp.max(masked, axis=-1, keepdims=True)
        idx = jnp.min(jnp.where(masked == m, lane, big), axis=-1, keepdims=True)
        sel = lane == idx
        sels.append(sel)
        wk = jnp.sum(jnp.where(sel, scores, 0.0), axis=-1, keepdims=True)
        idx_out = jnp.where(lane == k, idx, idx_out)
        w_out = jnp.where(lane == k, wk, w_out)
        onehot = jnp.where(sel, 1.0, onehot)
        masked = jnp.where(sel, NEG_INF, masked)
    w_out = w_out / jnp.sum(w_out, axis=-1, keepdims=True) * ROUTED_SCALE

    ri = lax.broadcasted_iota(I32, (tm, tm), 0)
    ci = lax.broadcasted_iota(I32, (tm, tm), 1)
    before = _dot((ri > ci).astype(BF16), onehot.astype(BF16)) + carry_s[...]
    rank_out = jnp.zeros((tm, LANES), F32)
    for k in range(TOP_K):
        rk = jnp.sum(jnp.where(sels[k], before, 0.0), axis=-1, keepdims=True)
        rank_out = jnp.where(lane == k, rk, rank_out)
    carry_s[...] = carry_s[...] + jnp.sum(onehot, axis=0, keepdims=True)

    idx_ref[...] = idx_out
    w_ref[...] = w_out
    rank_ref[...] = rank_out.astype(I32)
    cnt_ref[...] = jnp.broadcast_to(carry_s[...], cnt_ref.shape).astype(I32)


def _router(h_packed, router_w, router_bias):
    t, half = h_packed.shape
    row = pl.BlockSpec((ROW_TILE, LANES), lambda i: (i, 0))
    return pl.pallas_call(
        _router_body,
        grid=(t // ROW_TILE,),
        in_specs=[pl.BlockSpec((ROW_TILE, half), lambda i: (i, 0)),
                  pl.BlockSpec((2 * half, N_EXPERTS), lambda i: (0, 0)),
                  pl.BlockSpec((1, N_EXPERTS), lambda i: (0, 0))],
        out_specs=[row, row, row, pl.BlockSpec((SUBLANES, LANES), lambda i: (0, 0))],
        out_shape=[jax.ShapeDtypeStruct((t, LANES), I32), jax.ShapeDtypeStruct((t, LANES), F32),
                   jax.ShapeDtypeStruct((t, LANES), I32), jax.ShapeDtypeStruct((SUBLANES, LANES), I32)],
        scratch_shapes=[pltpu.VMEM((1, LANES), F32)],
        compiler_params=_cparams(("arbitrary",)),
    )(h_packed, router_w, router_bias)


def _gather_body(idx_ref, tab_ref, out_ref, sem, *, rows):
    base = pl.program_id(0) * rows

    def issue(r, carry):
        pltpu.make_async_copy(tab_ref.at[pl.ds(idx_ref[0, 0, r], 1)], out_ref.at[pl.ds(base + r, 1)], sem).start()
        return carry

    def drain(r, carry):
        pltpu.make_async_copy(tab_ref.at[pl.ds(0, 1)], out_ref.at[pl.ds(base, 1)], sem).wait()
        return carry

    lax.fori_loop(0, rows, issue, 0)
    lax.fori_loop(0, rows, drain, 0)


def _row_gather(table, idx):
    n = idx.shape[0]
    rows = GATHER_ROWS
    assert n % rows == 0
    return pl.pallas_call(
        functools.partial(_gather_body, rows=rows),
        grid=(n // rows,),
        in_specs=[pl.BlockSpec((1, 1, rows), lambda i: (i, 0, 0), memory_space=pltpu.SMEM),
                  pl.BlockSpec(memory_space=pl.ANY)],
        out_specs=pl.BlockSpec(memory_space=pl.ANY),
        out_shape=jax.ShapeDtypeStruct((n, table.shape[1]), table.dtype),
        scratch_shapes=[pltpu.SemaphoreType.DMA(())],
        compiler_params=_cparams(("arbitrary",)),
    )(idx.reshape(n // rows, 1, rows), table)


def _new_expert(be_ref, i):
    return (i == 0) | (be_ref[i] != be_ref[jnp.maximum(i - 1, 0)])


def _ffn_up_body(be_ref, bv_ref, x_ref, wg_ref, wu_ref, o_ref, wg_s, wu_s):
    i = pl.program_id(1)

    @pl.when(_new_expert(be_ref, i))
    def _():
        wg_s[...] = wg_ref[...].astype(BF16)
        wu_s[...] = wu_ref[...].astype(BF16)

    @pl.when(bv_ref[i] > 0)
    def _():
        lo, hi = _unpack_bf16_pairs(x_ref[...])
        half = lo.shape[1]
        g = _dot(lo, wg_s[:half]) + _dot(hi, wg_s[half:])
        u = _dot(lo, wu_s[:half]) + _dot(hi, wu_s[half:])
        o_ref[...] = (g * _sigmoid(g) * u).astype(o_ref.dtype)

    @pl.when(bv_ref[i] == 0)
    def _():
        o_ref[...] = jnp.zeros(o_ref.shape, o_ref.dtype)


def _ffn_down_body(be_ref, bv_ref, a_ref, wd_ref, o_ref, wd_s):
    i = pl.program_id(1)

    @pl.when(_new_expert(be_ref, i))
    def _():
        wd_s[...] = wd_ref[...].astype(BF16)

    @pl.when(bv_ref[i] > 0)
    def _():
        o_ref[...] = _dot(a_ref[...], wd_s[...])

    @pl.when(bv_ref[i] == 0)
    def _():
        o_ref[...] = jnp.zeros(o_ref.shape, o_ref.dtype)


def _expert_ffn(x_packed, block_e, block_valid, block_row, w_gate, w_up, w_down, layer, tm, tf, tn):
    n_rows, half = x_packed.shape
    d = 2 * half
    ff = w_gate.shape[-1]
    nb = block_e.shape[0]
    up = pl.pallas_call(
        _ffn_up_body,
        grid_spec=pltpu.PrefetchScalarGridSpec(
            num_scalar_prefetch=2,
            grid=(ff // tf, nb),
            in_specs=[pl.BlockSpec((tm, half), lambda j, i, be, bv: (bv[nb + i], 0)),
                      pl.BlockSpec((None, None, d, tf), lambda j, i, be, bv: (layer, be[i], 0, j)),
                      pl.BlockSpec((None, None, d, tf), lambda j, i, be, bv: (layer, be[i], 0, j))],
            out_specs=pl.BlockSpec((tm, tf), lambda j, i, be, bv: (i, j)),
            scratch_shapes=[pltpu.VMEM((d, tf), BF16), pltpu.VMEM((d, tf), BF16)],
        ),
        out_shape=jax.ShapeDtypeStruct((n_rows, ff), BF16),
        compiler_params=_cparams(("arbitrary", "arbitrary")),
    )
    bv = jnp.concatenate([block_valid, block_row])
    act = up(block_e, bv, x_packed, w_gate, w_up)
    down = pl.pallas_call(
        _ffn_down_body,
        grid_spec=pltpu.PrefetchScalarGridSpec(
            num_scalar_prefetch=2,
            grid=(d // tn, nb),
            in_specs=[pl.BlockSpec((tm, ff), lambda j, i, be, bv: (bv[nb + i], 0)),
                      pl.BlockSpec((None, None, ff, tn), lambda j, i, be, bv: (layer, be[i], 0, j))],
            out_specs=pl.BlockSpec((tm, tn), lambda j, i, be, bv: (i, j)),
            scratch_shapes=[pltpu.VMEM((ff, tn), BF16)],
        ),
        out_shape=jax.ShapeDtypeStruct((n_rows, d), F32),
        compiler_params=_cparams(("arbitrary", "arbitrary")),
    )
    return down(block_e, bv, act, w_down)


def _moe(h_packed, x1, p, layer):
    t, half = h_packed.shape
    top_idx, top_w, rank, counts = _router(h_packed, p["router_w"], p["router_bias"])
    tm = MOE_ROW_BLOCK
    counts = counts[0]
    padded = (counts + tm - 1) // tm * tm
    pad_end = jnp.cumsum(padded)
    pad_start = pad_end - padded
    pos = (pad_start[top_idx[:, :TOP_K]] + rank[:, :TOP_K]).reshape(-1)
    n_blocks = (t * TOP_K) // tm + N_EXPERTS
    n_slots = n_blocks * tm
    slot_tok = jnp.zeros((n_slots,), I32).at[pos].set(jnp.repeat(jnp.arange(t, dtype=I32), TOP_K))
    starts = jnp.arange(n_blocks, dtype=I32) * tm
    n_used = pad_end[-1] // tm
    valid = (starts < pad_end[-1]).astype(I32)
    blk = jnp.minimum(jnp.arange(n_blocks, dtype=I32), n_used - 1)
    block_e = jnp.minimum(jnp.searchsorted(pad_end, blk * tm, side="right"), N_EXPERTS - 1).astype(I32)

    x_sorted = _row_gather(h_packed, slot_tok)
    y_sorted = _expert_ffn(x_sorted, block_e, valid, blk, p["exp_gate"], p["exp_up"], p["exp_down"], layer,
                           tm=tm, tf=256, tn=1024)
    y_tok = _row_gather(y_sorted, pos).reshape(t, -1)

    stm = _pick_tm(t, 1024)
    nsb = t // stm
    one = jnp.ones((nsb,), I32)
    sblk = jnp.arange(nsb, dtype=I32)
    lead = lambda w: w.reshape((w.shape[0], 1) + w.shape[1:])
    shared = _expert_ffn(h_packed, 0 * one, one, sblk, lead(p["sh_gate"]), lead(p["sh_up"]), lead(p["sh_down"]),
                         layer, tm=stm, tf=256, tn=1024)
    return y_tok, top_w, shared


def kernel(x_prompt, x_sample, cache_k, cache_v, state_gdn, state_conv, page_table, c_prompt, c_sample, rel_bias, ada_w, ada_b, w_in, lam_q1, lam_k1, lam_q2, lam_k2, subln_w, conv_w, a_log, dt_bias, gdn_norm_w, w_branch, w_o, ln1_g, ln1_b, router_w, router_bias, exp_gate, exp_up, exp_down, sh_gate, sh_up, sh_down, ln2_g, ln2_b):
    bsz, seq, d = x_prompt.shape
    bs = x_sample.shape[0]
    depth = ada_w.shape[0]
    tp = bsz * seq
    t = tp + bs
    n_pages = page_table.shape[1]
    page = cache_k.shape[2]
    past = n_pages * page
    alpha = (2 * depth) ** 0.25
    assert bs == ROW_TILE and seq % ATTN_BLOCK == 0 and x_sample.shape[1] == 1

    x = jnp.concatenate([x_prompt.reshape(tp, d), x_sample.reshape(bs, d)], axis=0)
    c_all = jnp.concatenate([c_prompt, c_sample, jnp.zeros((-(bsz + bs) % SUBLANES, d), F32)], axis=0)

    q_w = N_ATTN_HEADS * 2 * ATTN_HD
    kv_w = N_KV_HEADS * 2 * ATTN_HD
    hw = GDN_HEADS * GDN_DK
    gdn0 = q_w + 2 * kv_w
    main_w = gdn0 + 4 * hw
    ba_w = 2 * GDN_HEADS

    blk = ATTN_BLOCK
    rel_tile = jnp.arange(blk)[:, None] - jnp.arange(blk)[None, :]
    rel2 = jnp.stack([rel_tile, rel_tile + blk])
    bias_tiles = jnp.where(rel2[None] >= 0, jnp.transpose(rel_bias[_t5_bucket(rel2)].astype(F32), (3, 0, 1, 2)),
                           NEG_INF)
    bias_far = rel_bias[N_BUCKETS - 1].astype(F32)
    rel_dec = past - jnp.arange(past)
    dec = rel_bias[_t5_bucket(rel_dec)].astype(F32).reshape(n_pages, page, N_KV_HEADS, 1, ATTN_GROUP)
    sbias = jnp.broadcast_to(jnp.transpose(dec, (0, 2, 3, 4, 1)),
                             (n_pages, N_KV_HEADS, 2, ATTN_GROUP, page)).reshape(n_pages, N_KV_HEADS, 2 * ATTN_GROUP, page)
    self_b = rel_bias[0].astype(F32).reshape(N_KV_HEADS, 1, ATTN_GROUP)
    sbias_self = jnp.broadcast_to(jnp.broadcast_to(self_b, (N_KV_HEADS, 2, ATTN_GROUP)).reshape(
        N_KV_HEADS, 2 * ATTN_GROUP, 1), (N_KV_HEADS, 2 * ATTN_GROUP, LANES))

    y = x
    outs = {n: [] for n in ("kp", "vp", "sp", "cp", "ks", "vs", "ss", "cs")}
    for l in range(depth):
        lam_init = 0.8 - 0.6 * math.exp(-0.3 * l)
        p = {"router_w": router_w[l], "router_bias": router_bias[l].reshape(1, -1), "exp_gate": exp_gate,
             "exp_up": exp_up, "exp_down": exp_down, "sh_gate": sh_gate, "sh_up": sh_up, "sh_down": sh_down}

        mod = _mm(c_all, ada_w[l], tm=c_all.shape[0], tn=512, n_out=6 * d, bias=ada_b[l].reshape(1, -1),
                  a_silu=True)
        mod_p = mod[:bsz].reshape(bsz, 1, 6 * d)
        mod_s = mod[bsz:bsz + bs]

        h = _modulate(y, mod_p, mod_s, 1, 0, seq)
        tm_big = _pick_tm(t, 1664)
        proj = _mm(h, w_in[l], tm=tm_big, tn=256, n_out=main_w)
        w_tail = w_in[l][:, main_w:]
        w_ba = jnp.pad(w_tail[:, :ba_w], ((0, 0), (0, LANES - ba_w)))
        ba = _mm(h, w_ba, tm=tm_big, tn=LANES, n_out=LANES)
        gates = _mm(h, w_tail[:, ba_w:], tm=tm_big, tn=256, n_out=2 * d)

        lam_vecs = jnp.stack([lam_q1[l], lam_k1[l], lam_q2[l], lam_k2[l]]).astype(F32)
        sub_w = subln_w[l].reshape(1, -1).astype(F32)

        o_a_p = _flash_attention(proj, bias_tiles, bias_far, lam_vecs, sub_w, bsz, seq, lam_init)
        srow = proj[tp:]
        q_s = srow[:, :q_w].reshape(bs, N_KV_HEADS, ATTN_GROUP, 2, ATTN_HD)
        zeros = jnp.zeros_like(q_s[:, :, :, 0])
        qbd = jnp.stack([jnp.concatenate([q_s[:, :, :, 0], zeros], -1),
                         jnp.concatenate([zeros, q_s[:, :, :, 1]], -1)], axis=2)
        qbd = qbd.reshape(bs, N_KV_HEADS, 2 * ATTN_GROUP, 2 * ATTN_HD)
        k_s = srow[:, q_w:q_w + kv_w]
        v_s = srow[:, q_w + kv_w:gdn0]
        o_a_s = _paged_attention(page_table, qbd, k_s.reshape(bs, N_KV_HEADS, 1, 2 * ATTN_HD),
                                 v_s.reshape(bs, N_KV_HEADS, 1, ATTN_VD), cache_k, cache_v, l, sbias,
                                 sbias_self, lam_vecs, sub_w, lam_init)
        o_a = jnp.concatenate([o_a_p, o_a_s.reshape(bs, -1).astype(BF16)], axis=0)

        n_hg = GDN_HEADS // GDN_HEAD_GROUP
        group_lanes = lambda vec: jnp.pad(vec.astype(F32).reshape(n_hg, 1, GDN_HEAD_GROUP),
                                          ((0, 0), (0, 0), (0, LANES - GDN_HEAD_GROUP)))
        nw = gdn_norm_w[l].reshape(1, -1).astype(F32)
        o_g_p, s_p = _gdn_prompt(proj, ba, conv_w[l], group_lanes(a_log[l]), group_lanes(dt_bias[l]), nw,
                                 bsz, seq, gdn0)
        nr = 3 * GDN_HEADS
        o_g_s, s_s, c_s = _gdn_step(
            srow[:, gdn0:gdn0 + 3 * hw].reshape(bs, nr, GDN_DK),
            state_conv[l].reshape(bs, GDN_CONV - 1, nr, GDN_DK),
            conv_w[l].reshape(GDN_CONV, nr, GDN_DK),
            srow[:, gdn0 + 3 * hw:main_w].reshape(bs, GDN_HEADS, GDN_DV),
            ba[tp:, :GDN_HEADS].reshape(bs, GDN_HEADS, 1), ba[tp:, GDN_HEADS:ba_w].reshape(bs, GDN_HEADS, 1),
            a_log[l].reshape(-1, 1).astype(F32), dt_bias[l].reshape(-1, 1).astype(F32), nw, state_gdn[l])
        o_g = jnp.concatenate([o_g_p, o_g_s.reshape(bs, -1).astype(BF16)], axis=0)

        tm_mid = _pick_tm(t, 1024)
        m_a = _mm(o_a, w_branch[l], tm=tm_mid, tn=512, n_out=d, gate=gates)
        merged = _mm(o_g, w_branch[l], tm=tm_mid, tn=512, n_out=d, w_row_blk=1, gate=gates,
                     gate_col_blk=d // 512, prev=m_a, out_dtype=BF16)
        attn_out = _mm(merged, w_o[l], tm=tm_mid, tn=512, n_out=d)
        x1, h2p = _ln1(y, attn_out, mod_p, mod_s, ln1_g[l].reshape(1, -1), ln1_b[l].reshape(1, -1), seq, alpha)

        y_tok, top_w, shared = _moe(h2p, x1, p, l)
        y = _final(x1, y_tok, top_w, shared, mod_p, mod_s, ln2_g[l].reshape(1, -1), ln2_b[l].reshape(1, -1),
                   seq, alpha)

        kcols = proj[:, q_w:q_w + kv_w]
        vcols = proj[:, q_w + kv_w:gdn0]
        outs["kp"].append(kcols[:tp].reshape(bsz, seq, N_KV_HEADS, 2 * ATTN_HD))
        outs["vp"].append(vcols[:tp].reshape(bsz, seq, N_KV_HEADS, ATTN_VD))
        outs["sp"].append(s_p)
        outs["cp"].append(proj[:tp, gdn0:gdn0 + 3 * hw].reshape(bsz, seq, 3 * hw)[:, seq - (GDN_CONV - 1):])
        outs["ks"].append(kcols[tp:].reshape(bs, 1, N_KV_HEADS, 2 * ATTN_HD))
        outs["vs"].append(vcols[tp:].reshape(bs, 1, N_KV_HEADS, ATTN_VD))
        outs["ss"].append(s_s)
        outs["cs"].append(c_s.reshape(bs, GDN_CONV - 1, 3 * hw))

    st = lambda n: jnp.stack(outs[n])
    return (y[:tp].reshape(bsz, seq, d), y[tp:].reshape(bs, 1, d), st("kp"), st("vp"), st("sp"), st("cp"),
            st("ks"), st("vs"), st("ss"), st("cs"))
```

```python
import functools
import math

import jax
import jax.numpy as jnp
from jax import lax
from jax.experimental import pallas as pl
from jax.experimental.pallas import tpu as pltpu

F32 = jnp.float32
BF16 = jnp.bfloat16
U32 = jnp.uint32
I32 = jnp.int32

N_ATTN_HEADS = 16
N_KV_HEADS = 4
ATTN_GROUP = N_ATTN_HEADS // N_KV_HEADS
ATTN_HD = 128
ATTN_VD = 256
N_BUCKETS = 32
MAX_DISTANCE = 128
GDN_HEADS = 32
GDN_DK = 128
GDN_DV = 128
GDN_CONV = 4
GDN_CHUNK = 64
N_EXPERTS = 128
TOP_K = 8
N_GROUPS = 8
GROUP_SIZE = N_EXPERTS // N_GROUPS
TOPK_GROUPS = 4
ROUTED_SCALE = 2.5

LANES = 128
SUBLANES = 8
VMEM_LIMIT = 56 * 1024 * 1024

ROW_TILE = 128
FINAL_TILE = 64
ATTN_BLOCK = 256
PAGES_PER_STEP = 4
GDN_HEAD_GROUP = 8
GDN_STACK = 4
MOE_ROW_BLOCK = 256
GATHER_ROWS = 256

NEG_INF = float("-inf")


def _cparams(sem, vmem=VMEM_LIMIT):
    return pltpu.CompilerParams(dimension_semantics=sem, vmem_limit_bytes=vmem)


def _sigmoid(x):
    return jax.nn.sigmoid(x)


def _dot(a, b):
    return jnp.dot(a, b, preferred_element_type=F32)


def _dot_nt(a, b):
    return lax.dot_general(a, b, (((1,), (1,)), ((), ())), preferred_element_type=F32)


def _dot_tn(a, b):
    return lax.dot_general(a, b, (((0,), (0,)), ((), ())), preferred_element_type=F32)


def _mm_body(*refs, a_silu, has_bias, has_gate, has_prev):
    it = iter(refs)
    a_ref = next(it)
    w_ref = next(it)
    bias_ref = next(it) if has_bias else None
    gate_ref = next(it) if has_gate else None
    prev_ref = next(it) if has_prev else None
    o_ref = next(it)
    a = a_ref[...]
    if a_silu:
        a = a.astype(F32)
        a = a * _sigmoid(a)
    acc = _dot(a.astype(BF16), w_ref[...].astype(BF16))
    if has_bias:
        acc = acc + bias_ref[...]
    if has_gate:
        acc = _sigmoid(gate_ref[...]) * acc
    if has_prev:
        acc = acc + prev_ref[...]
    o_ref[...] = acc.astype(o_ref.dtype)


def _mm(a, w, *, tm, tn, n_out, w_row_blk=0, w_col_blk=0, bias=None, gate=None, gate_col_blk=0,
        prev=None, a_silu=False, out_dtype=F32):
    m, k = a.shape
    assert m % tm == 0 and n_out % tn == 0
    in_specs = [pl.BlockSpec((tm, k), lambda i, j: (i, 0)),
                pl.BlockSpec((k, tn), lambda i, j: (w_row_blk, j + w_col_blk))]
    args = [a, w]
    if bias is not None:
        in_specs.append(pl.BlockSpec((1, tn), lambda i, j: (0, j)))
        args.append(bias)
    if gate is not None:
        in_specs.append(pl.BlockSpec((tm, tn), lambda i, j: (i, j + gate_col_blk)))
        args.append(gate)
    if prev is not None:
        in_specs.append(pl.BlockSpec((tm, tn), lambda i, j: (i, j)))
        args.append(prev)
    body = functools.partial(_mm_body, a_silu=a_silu, has_bias=bias is not None,
                             has_gate=gate is not None, has_prev=prev is not None)
    return pl.pallas_call(
        body,
        grid=(m // tm, n_out // tn),
        in_specs=in_specs,
        out_specs=pl.BlockSpec((tm, tn), lambda i, j: (i, j)),
        out_shape=jax.ShapeDtypeStruct((m, n_out), out_dtype),
        compiler_params=_cparams(("parallel", "arbitrary")),
    )(*args)


def _pick_tm(m, cap):
    best = None
    for t in range(LANES, cap + 1, LANES):
        if m % t == 0:
            best = t
    assert best is not None
    return best


def _row_mod(i, n_prompt_tiles, p_ref, s_ref):
    return jnp.where(i >= n_prompt_tiles, s_ref[...], p_ref[0])


def _mod_specs(d, chunk, tiles_per_seq, n_prompt_tiles, tile=ROW_TILE):
    p_spec = pl.BlockSpec((1, 1, d), lambda i: (jnp.minimum(i, n_prompt_tiles - 1) // tiles_per_seq, 0, chunk))
    s_spec = pl.BlockSpec((tile, d), lambda i: (jnp.maximum(i - n_prompt_tiles, 0), chunk))
    return p_spec, s_spec


def _modulate_body(x_ref, scp_ref, scs_ref, shp_ref, shs_ref, o_ref, *, n_prompt_tiles):
    i = pl.program_id(0)
    sc = _row_mod(i, n_prompt_tiles, scp_ref, scs_ref)
    sh = _row_mod(i, n_prompt_tiles, shp_ref, shs_ref)
    o_ref[...] = (x_ref[...] * (1.0 + sc) + sh).astype(o_ref.dtype)


def _modulate(x, mod_p, mod_s, sc_chunk, sh_chunk, seq):
    t, d = x.shape
    n_prompt_tiles = mod_p.shape[0] * seq // ROW_TILE
    tps = seq // ROW_TILE
    scp, scs = _mod_specs(d, sc_chunk, tps, n_prompt_tiles)
    shp, shs = _mod_specs(d, sh_chunk, tps, n_prompt_tiles)
    return pl.pallas_call(
        functools.partial(_modulate_body, n_prompt_tiles=n_prompt_tiles),
        grid=(t // ROW_TILE,),
        in_specs=[pl.BlockSpec((ROW_TILE, d), lambda i: (i, 0)), scp, scs, shp, shs],
        out_specs=pl.BlockSpec((ROW_TILE, d), lambda i: (i, 0)),
        out_shape=jax.ShapeDtypeStruct((t, d), BF16),
        compiler_params=_cparams(("parallel",)),
    )(x, mod_p, mod_s, mod_p, mod_s)


def _layer_norm_rows(v, g, b):
    mu = jnp.mean(v, -1, keepdims=True)
    var = jnp.mean(jnp.square(v - mu), -1, keepdims=True)
    return (v - mu) * lax.rsqrt(var + 1e-5) * g + b


def _pack_bf16_pairs(h):
    half = h.shape[1] // 2
    bits = lax.bitcast_convert_type(h.astype(BF16).astype(F32), U32)
    return (bits[:, :half] >> 16) | (bits[:, half:] & jnp.uint32(0xFFFF0000))


def _unpack_bf16_pairs(p):
    lo = lax.bitcast_convert_type(p << 16, F32).astype(BF16)
    hi = lax.bitcast_convert_type(p & jnp.uint32(0xFFFF0000), F32).astype(BF16)
    return lo, hi


def _store_slabs(o_ref, v):
    rows, n = v.shape[0], v.shape[1] // LANES
    for s in range(n):
        o_ref[pl.ds(s, rows, stride=n), :] = v[:, s * LANES:(s + 1) * LANES]


def _load_slabs(ref, rows, n, pitch=None, base=0):
    pitch = n if pitch is None else pitch
    return jnp.concatenate([ref[pl.ds(base + s, rows, stride=pitch), :] for s in range(n)], axis=1)


def _ln1_body(x_ref, y_ref, gp_ref, gs_ref, scp_ref, scs_ref, shp_ref, shs_ref, lg_ref, lb_ref,
              x1_ref, hp_ref, *, n_prompt_tiles, alpha):
    i = pl.program_id(0)
    g1 = _row_mod(i, n_prompt_tiles, gp_ref, gs_ref)
    sc = _row_mod(i, n_prompt_tiles, scp_ref, scs_ref)
    sh = _row_mod(i, n_prompt_tiles, shp_ref, shs_ref)
    x1 = _layer_norm_rows(alpha * x_ref[...] + g1 * y_ref[...], lg_ref[...], lb_ref[...])
    x1_ref[...] = x1
    _store_slabs(hp_ref, _pack_bf16_pairs(x1 * (1.0 + sc) + sh))


def _ln1(x, y, mod_p, mod_s, ln_g, ln_b, seq, alpha):
    t, d = x.shape
    n_prompt_tiles = mod_p.shape[0] * seq // ROW_TILE
    tps = seq // ROW_TILE
    gp, gs = _mod_specs(d, 2, tps, n_prompt_tiles)
    shp, shs = _mod_specs(d, 3, tps, n_prompt_tiles)
    scp, scs = _mod_specs(d, 4, tps, n_prompt_tiles)
    row = pl.BlockSpec((ROW_TILE, d), lambda i: (i, 0))
    vec = pl.BlockSpec((1, d), lambda i: (0, 0))
    return pl.pallas_call(
        functools.partial(_ln1_body, n_prompt_tiles=n_prompt_tiles, alpha=alpha),
        grid=(t // ROW_TILE,),
        in_specs=[row, row, gp, gs, scp, scs, shp, shs, vec, vec],
        out_specs=[row, pl.BlockSpec((ROW_TILE * (d // 2 // LANES), LANES), lambda i: (i, 0))],
        out_shape=[jax.ShapeDtypeStruct((t, d), F32), jax.ShapeDtypeStruct((t * (d // 2 // LANES), LANES), U32)],
        compiler_params=_cparams(("parallel",)),
    )(x, y, mod_p, mod_s, mod_p, mod_s, mod_p, mod_s, ln_g, ln_b)


def _final_body(x_ref, yt_ref, tw_ref, sh_ref, gp_ref, gs_ref, lg_ref, lb_ref, o_ref, *,
                n_prompt_tiles, alpha, d):
    i = pl.program_id(0)
    g2 = _row_mod(i, n_prompt_tiles, gp_ref, gs_ref)
    f = sh_ref[...]
    tw = tw_ref[...]
    rows, n = f.shape[0], d // LANES
    for k in range(TOP_K):
        f = f + _load_slabs(yt_ref, rows, n, pitch=TOP_K * n, base=k * n) * tw[:, k:k + 1]
    o_ref[...] = _layer_norm_rows(alpha * x_ref[...] + g2 * f, lg_ref[...], lb_ref[...])


def _final(x1, y_tok, top_w, shared, mod_p, mod_s, ln_g, ln_b, seq, alpha):
    t, d = x1.shape
    tile = FINAL_TILE
    n_prompt_tiles = mod_p.shape[0] * seq // tile
    gp, gs = _mod_specs(d, 5, seq // tile, n_prompt_tiles, tile)
    row = pl.BlockSpec((tile, d), lambda i: (i, 0))
    vec = pl.BlockSpec((1, d), lambda i: (0, 0))
    return pl.pallas_call(
        functools.partial(_final_body, n_prompt_tiles=n_prompt_tiles, alpha=alpha, d=d),
        grid=(t // tile,),
        in_specs=[row, pl.BlockSpec((tile * TOP_K * (d // LANES), LANES), lambda i: (i, 0)),
                  pl.BlockSpec((tile, LANES), lambda i: (i, 0)), row, gp, gs, vec, vec],
        out_specs=row,
        out_shape=jax.ShapeDtypeStruct((t, d), F32),
        compiler_params=_cparams(("parallel",)),
    )(x1, y_tok, top_w, shared, mod_p, mod_s, ln_g, ln_b)


def _t5_bucket(rel):
    n = jnp.maximum(rel, 0)
    max_exact = N_BUCKETS // 2
    nf = jnp.maximum(n, 1).astype(F32)
    large = max_exact + (jnp.log(nf / max_exact) / math.log(MAX_DISTANCE / max_exact)
                         * (N_BUCKETS - max_exact)).astype(I32)
    return jnp.where(n < max_exact, n, jnp.minimum(large, N_BUCKETS - 1))


def _lambda(lam_ref, lam_init):
    lam = lam_ref[...]
    s1 = jnp.sum(lam[0:1] * lam[1:2], axis=-1, keepdims=True)
    s2 = jnp.sum(lam[2:3] * lam[3:4], axis=-1, keepdims=True)
    return jnp.exp(s1) - jnp.exp(s2) + lam_init


def _sub_norm(o, w, lam_init):
    return o * lax.rsqrt(jnp.mean(o * o, -1, keepdims=True) + 1e-5) * w * (1.0 - lam_init)


def _online_update(idx, s, v, m_s, l_s, acc_s):
    m_old = m_s[idx]
    m_new = jnp.maximum(m_old, jnp.max(s, axis=-1, keepdims=True))
    p = jnp.exp(s - m_new)
    corr = jnp.exp(m_old - m_new)
    l_s[idx] = corr * l_s[idx] + jnp.sum(p, axis=-1, keepdims=True)
    acc_s[idx] = corr * acc_s[idx] + _dot(p.astype(BF16), v)
    m_s[idx] = m_new


def _flash_body(far_ref, q_ref, k_ref, v_ref, bt_ref, lam_ref, sub_ref, o_ref, m_s, l_s, acc_s, *,
                scale, lam_init):
    kv = pl.program_id(1)
    qi = pl.program_id(2)
    ki = pl.program_id(3)
    hd, vd, grp = ATTN_HD, ATTN_VD, ATTN_GROUP

    @pl.when(ki == 0)
    def _():
        m_s[...] = jnp.full(m_s.shape, NEG_INF, F32)
        l_s[...] = jnp.zeros(l_s.shape, F32)
        acc_s[...] = jnp.zeros(acc_s.shape, F32)

    def process(get_bias):
        k = k_ref[...].astype(BF16)
        v = v_ref[...].astype(BF16)
        for g in range(grp):
            bias = get_bias(g)
            for j in range(2):
                c0 = g * 2 * hd + j * hd
                q = q_ref[:, c0:c0 + hd].astype(BF16)
                s = _dot_nt(q, k[:, j * hd:(j + 1) * hd]) * scale + bias
                _online_update(g * 2 + j, s, v, m_s, l_s, acc_s)

    @pl.when(ki < qi - 1)
    def _():
        process(lambda g: far_ref[kv * grp + g])

    @pl.when(ki == qi - 1)
    def _():
        process(lambda g: bt_ref[g, 1])

    @pl.when(ki == qi)
    def _():
        process(lambda g: bt_ref[g, 0])
        lam = _lambda(lam_ref, lam_init)
        for g in range(grp):
            o1 = acc_s[g * 2] / l_s[g * 2]
            o2 = acc_s[g * 2 + 1] / l_s[g * 2 + 1]
            o = _sub_norm(o1 - lam * o2, sub_ref[...], lam_init)
            o_ref[:, g * vd:(g + 1) * vd] = o.astype(o_ref.dtype)


def _flash_attention(proj, bias_tiles, bias_far, lam_vecs, subln_w, bsz, seq, lam_init):
    blk = ATTN_BLOCK
    nblk = seq // blk
    grp, hd, vd = ATTN_GROUP, ATTN_HD, ATTN_VD
    qw = grp * 2 * hd
    k_col0 = N_KV_HEADS * qw // (2 * hd)
    v_col0 = k_col0 + N_KV_HEADS
    grid_spec = pltpu.PrefetchScalarGridSpec(
        num_scalar_prefetch=0,
        grid=(bsz, N_KV_HEADS, nblk, nblk),
        in_specs=[
            pl.BlockSpec(memory_space=pltpu.SMEM),
            pl.BlockSpec((blk, qw), lambda b, kv, qi, ki: (b * nblk + qi, kv)),
            pl.BlockSpec((blk, 2 * hd), lambda b, kv, qi, ki: (b * nblk + jnp.minimum(ki, qi), k_col0 + kv)),
            pl.BlockSpec((blk, vd), lambda b, kv, qi, ki: (b * nblk + jnp.minimum(ki, qi), v_col0 + kv)),
            pl.BlockSpec((grp, 2, blk, blk), lambda b, kv, qi, ki: (kv, 0, 0, 0)),
            pl.BlockSpec((4, hd), lambda b, kv, qi, ki: (0, 0)),
            pl.BlockSpec((1, vd), lambda b, kv, qi, ki: (0, 0)),
        ],
        out_specs=pl.BlockSpec((blk, grp * vd), lambda b, kv, qi, ki: (b * nblk + qi, kv)),
        scratch_shapes=[pltpu.VMEM((2 * grp, blk, 1), F32), pltpu.VMEM((2 * grp, blk, 1), F32),
                        pltpu.VMEM((2 * grp, blk, vd), F32)],
    )
    return pl.pallas_call(
        functools.partial(_flash_body, scale=hd ** -0.5, lam_init=lam_init),
        grid_spec=grid_spec,
        out_shape=jax.ShapeDtypeStruct((bsz * seq, N_ATTN_HEADS * vd), BF16),
        compiler_params=_cparams(("parallel", "parallel", "parallel", "arbitrary")),
    )(bias_far, proj, proj, proj, bias_tiles, lam_vecs, subln_w)


def _paged_body(pt_ref, qbd_ref, kn_ref, vn_ref, sb_ref, sbs_ref, lam_ref, sub_ref, *rest, npp, scale,
                lam_init):
    k_refs = rest[:npp]
    v_refs = rest[npp:2 * npp]
    o_ref = rest[2 * npp]
    m_s, l_s, acc_s = rest[2 * npp + 1:]
    pg = pl.program_id(1)
    grp, vd = ATTN_GROUP, ATTN_VD

    @pl.when(pg == 0)
    def _():
        for kv in range(N_KV_HEADS):
            q = qbd_ref[0, kv].astype(BF16).astype(F32)
            kn = kn_ref[0, kv].astype(BF16).astype(F32)
            s = jnp.sum(q * kn, axis=-1, keepdims=True) * scale + sbs_ref[kv][:, 0:1]
            m_s[kv] = s
            l_s[kv] = jnp.ones(s.shape, F32)
            acc_s[kv] = jnp.broadcast_to(vn_ref[0, kv].astype(BF16).astype(F32), (2 * grp, vd))

    for i in range(npp):
        for kv in range(N_KV_HEADS):
            q = qbd_ref[0, kv].astype(BF16)
            k = k_refs[i][:, kv, :].astype(BF16)
            v = v_refs[i][:, kv, :].astype(BF16)
            s = _dot_nt(q, k) * scale + sb_ref[pg * npp + i, kv]
            _online_update(kv, s, v, m_s, l_s, acc_s)

    @pl.when(pg == pl.num_programs(1) - 1)
    def _():
        lam = _lambda(lam_ref, lam_init)
        for kv in range(N_KV_HEADS):
            n = acc_s[kv] / l_s[kv]
            o = _sub_norm(n[0:grp] - lam * n[grp:2 * grp], sub_ref[...], lam_init)
            for g in range(grp):
                c0 = (kv * grp + g) * vd
                o_ref[0, :, c0:c0 + vd] = o[g:g + 1]


def _paged_attention(page_table, qbd, k_new, v_new, cache_k, cache_v, layer, sbias, sbias_self, lam_vecs,
                     subln_w, lam_init):
    bs, n_pages = page_table.shape
    npp = math.gcd(PAGES_PER_STEP, n_pages)
    page = cache_k.shape[2]
    kw = 2 * ATTN_HD
    grp, vd = ATTN_GROUP, ATTN_VD

    def page_spec(width, i):
        return pl.BlockSpec((None, None, page, N_KV_HEADS, width),
                            lambda b, pg, pt: (layer, pt[b, pg * npp + i], 0, 0, 0))

    grid_spec = pltpu.PrefetchScalarGridSpec(
        num_scalar_prefetch=1,
        grid=(bs, n_pages // npp),
        in_specs=[
            pl.BlockSpec((1, N_KV_HEADS, 2 * grp, 2 * ATTN_HD), lambda b, pg, pt: (b, 0, 0, 0)),
            pl.BlockSpec((1, N_KV_HEADS, 1, 2 * ATTN_HD), lambda b, pg, pt: (b, 0, 0, 0)),
            pl.BlockSpec((1, N_KV_HEADS, 1, vd), lambda b, pg, pt: (b, 0, 0, 0)),
            pl.BlockSpec((n_pages, N_KV_HEADS, 2 * grp, page), lambda b, pg, pt: (0, 0, 0, 0)),
            pl.BlockSpec((N_KV_HEADS, 2 * grp, LANES), lambda b, pg, pt: (0, 0, 0)),
            pl.BlockSpec((4, ATTN_HD), lambda b, pg, pt: (0, 0)),
            pl.BlockSpec((1, vd), lambda b, pg, pt: (0, 0)),
        ] + [page_spec(kw, i) for i in range(npp)] + [page_spec(vd, i) for i in range(npp)],
        out_specs=pl.BlockSpec((1, 1, N_ATTN_HEADS * vd), lambda b, pg, pt: (b, 0, 0)),
        scratch_shapes=[pltpu.VMEM((N_KV_HEADS, 2 * grp, 1), F32), pltpu.VMEM((N_KV_HEADS, 2 * grp, 1), F32),
                        pltpu.VMEM((N_KV_HEADS, 2 * grp, vd), F32)],
    )
    return pl.pallas_call(
        functools.partial(_paged_body, npp=npp, scale=ATTN_HD ** -0.5, lam_init=lam_init),
        grid_spec=grid_spec,
        out_shape=jax.ShapeDtypeStruct((bs, 1, N_ATTN_HEADS * vd), F32),
        compiler_params=_cparams(("parallel", "arbitrary")),
    )(page_table, qbd, k_new, v_new, sbias, sbias_self, lam_vecs, subln_w, *([cache_k] * npp),
      *([cache_v] * npp))


def _softplus(x):
    return jnp.maximum(x, 0.0) + jnp.log(1.0 + jnp.exp(-jnp.abs(x)))


def _l2norm(x):
    return x * lax.rsqrt(jnp.sum(x * x, -1, keepdims=True) + 1e-6)


def _split_bf16(x):
    hi = x.astype(BF16)
    return hi, (x - hi.astype(F32)).astype(BF16)


def _unit_lower_inverse(a, nilpotency):
    n = a.shape[0]
    eye = (lax.broadcasted_iota(I32, (n, n), 0) == lax.broadcasted_iota(I32, (n, n), 1)).astype(F32)
    y = -a
    r = eye + y
    span = 2
    while span < nilpotency:
        y_hi, y_lo = _split_bf16(y)
        y = _dot(y_hi, y_hi) + (_dot(y_hi, y_lo) + _dot(y_lo, y_hi))
        y_hi, y_lo = _split_bf16(y)
        r_hi, r_lo = _split_bf16(r)
        r = r + (_dot(r_hi, y_hi) + (_dot(r_hi, y_lo) + _dot(r_lo, y_hi)))
        span *= 2
    return r


def _gdn_body(q_ref, k_ref, v_ref, qp_ref, kp_ref, vp_ref, z_ref, ba_ref, cwq_ref, cwk_ref, cwv_ref,
              alog_ref, dtb_ref, nw_ref, o_ref, sout_ref, s_s, *, hg_size, chunk):
    hg = pl.program_id(1)
    c = pl.program_id(2)
    first = c == 0
    dk, dv = GDN_DK, GDN_DV

    @pl.when(first)
    def _():
        s_s[...] = jnp.zeros(s_s.shape, F32)

    def conv_silu(cur_ref, prev_ref, w_ref):
        cur = cur_ref[...]
        prev = jnp.where(first, 0.0, prev_ref[...])
        ext = jnp.concatenate([prev, cur], axis=0)
        y = cur * w_ref[GDN_CONV - 1:GDN_CONV, :]
        for d in range(1, GDN_CONV):
            shifted = pltpu.roll(ext, d, 0)[SUBLANES:SUBLANES + chunk]
            y = y + shifted * w_ref[GDN_CONV - 1 - d:GDN_CONV - d, :]
        return y * _sigmoid(y)

    qc = conv_silu(q_ref, qp_ref, cwq_ref)
    kc = conv_silu(k_ref, kp_ref, cwk_ref)
    vc = conv_silu(v_ref, vp_ref, cwv_ref)

    raw = ba_ref[...]
    off = hg * hg_size
    b_raw = pltpu.roll(raw, (LANES - off) % LANES, 1)
    a_raw = pltpu.roll(raw, (2 * LANES - GDN_HEADS - off) % LANES, 1)
    beta_all = _sigmoid(b_raw)
    g_all = -jnp.exp(alog_ref[0]) * _softplus(a_raw + dtb_ref[0])
    rows = lax.broadcasted_iota(I32, g_all.shape, 0)
    gc_all = g_all
    span = 1
    while span < chunk:
        gc_all = gc_all + jnp.where(rows >= span, pltpu.roll(gc_all, span, 0), 0.0)
        span *= 2
    gc_t = gc_all.T
    eg_all = jnp.exp(gc_all)

    nst = GDN_STACK
    rows_g = nst * chunk
    ri = lax.broadcasted_iota(I32, (rows_g, rows_g), 0)
    ci = lax.broadcasted_iota(I32, (rows_g, rows_g), 1)
    same_head = (ri // chunk) == (ci // chunk)
    causal = same_head & (ri >= ci)
    strict = same_head & (ri > ci)

    def stack(x_all, width, h0):
        return jnp.concatenate([x_all[:, (h0 + s) * width:(h0 + s + 1) * width] for s in range(nst)], axis=0)

    z_all = z_ref[...]
    for h0 in range(0, hg_size, nst):
        q = _l2norm(stack(qc, dk, h0)) * dk ** -0.5
        k = _l2norm(stack(kc, dk, h0))
        v = stack(vc, dv, h0)
        beta = stack(beta_all, 1, h0)
        gcol = stack(gc_all, 1, h0)
        egc = stack(eg_all, 1, h0)
        grow = jnp.concatenate([gc_t[h0 + s:h0 + s + 1, :] for s in range(nst)], axis=1)
        decay = jnp.where(causal, jnp.exp(jnp.where(causal, gcol - grow, 0.0)), 0.0)
        kb = k * beta
        kbf = k.astype(BF16)
        p = _dot_nt(jnp.concatenate([kb, q], axis=0).astype(BF16), kbf)
        a_mat = jnp.where(strict, p[:rows_g] * decay, 0.0)
        qk = p[rows_g:] * decay
        t_mat = _unit_lower_inverse(a_mat, chunk).astype(BF16)
        uw = _dot(t_mat, jnp.concatenate([v * beta, kb * egc], axis=1).astype(BF16))
        u, w = uw[:, :dv], uw[:, dv:]
        qd = q * egc
        ws, qs = [], []
        for s in range(nst):
            r0, r1 = s * chunk, (s + 1) * chunk
            both = _dot(jnp.concatenate([w[r0:r1], qd[r0:r1]], axis=0).astype(BF16), s_s[h0 + s].astype(BF16))
            ws.append(both[:chunk])
            qs.append(both[chunk:])
        v_new = u - jnp.concatenate(ws, axis=0)
        v_new_bf = v_new.astype(BF16)
        o = jnp.concatenate(qs, axis=0) + _dot(qk.astype(BF16), v_new_bf)
        for s in range(nst):
            r0, r1 = s * chunk, (s + 1) * chunk
            g_last = gcol[r1 - 1:r1, :]
            k_dec = (k[r0:r1] * jnp.exp(g_last - gcol[r0:r1])).astype(BF16)
            s_s[h0 + s] = s_s[h0 + s] * jnp.exp(g_last) + _dot_tn(k_dec, v_new_bf[r0:r1])
        z = stack(z_all, dv, h0)
        on = o * lax.rsqrt(jnp.mean(o * o, -1, keepdims=True) + 1e-6) * nw_ref[...] * (z * _sigmoid(z))
        for s in range(nst):
            o_ref[:, (h0 + s) * dv:(h0 + s + 1) * dv] = on[s * chunk:(s + 1) * chunk].astype(o_ref.dtype)

    @pl.when(c == pl.num_programs(2) - 1)
    def _():
        sout_ref[0] = s_s[...]


def _gdn_prompt(proj, ba, conv_w, alog_g, dtb_g, norm_w, bsz, seq, col0):
    hgs, chunk = GDN_HEAD_GROUP, GDN_CHUNK
    n_hg = GDN_HEADS // hgs
    w = hgs * GDN_DK
    n_chunks = seq // chunk
    hw = GDN_HEADS * GDN_DK
    assert col0 % w == 0 and seq % chunk == 0
    qb, kb, vb, zb = (col0 // w + i * (hw // w) for i in range(4))
    cpt = chunk // SUBLANES

    def cur(base):
        return pl.BlockSpec((chunk, w), lambda b, hg, c: (b * n_chunks + c, base + hg))

    def prev(base):
        return pl.BlockSpec((SUBLANES, w),
                            lambda b, hg, c: (jnp.maximum((b * n_chunks + c) * cpt - 1, 0), base + hg))

    def cw(base):
        return pl.BlockSpec((GDN_CONV, w), lambda b, hg, c: (0, base + hg))

    return pl.pallas_call(
        functools.partial(_gdn_body, hg_size=hgs, chunk=chunk),
        grid=(bsz, n_hg, n_chunks),
        in_specs=[cur(qb), cur(kb), cur(vb), prev(qb), prev(kb), prev(vb), cur(zb),
                  pl.BlockSpec((chunk, LANES), lambda b, hg, c: (b * n_chunks + c, 0)),
                  cw(0), cw(hw // w), cw(2 * hw // w),
                  pl.BlockSpec((1, 1, LANES), lambda b, hg, c: (hg, 0, 0)),
                  pl.BlockSpec((1, 1, LANES), lambda b, hg, c: (hg, 0, 0)),
                  pl.BlockSpec((1, GDN_DV), lambda b, hg, c: (0, 0))],
        out_specs=[pl.BlockSpec((chunk, w), lambda b, hg, c: (b * n_chunks + c, hg)),
                   pl.BlockSpec((1, hgs, GDN_DK, GDN_DV), lambda b, hg, c: (b, hg, 0, 0))],
        out_shape=[jax.ShapeDtypeStruct((bsz * seq, hw), BF16),
                   jax.ShapeDtypeStruct((bsz, GDN_HEADS, GDN_DK, GDN_DV), F32)],
        scratch_shapes=[pltpu.VMEM((hgs, GDN_DK, GDN_DV), F32)],
        compiler_params=_cparams(("parallel", "parallel", "arbitrary")),
    )(proj, proj, proj, proj, proj, proj, proj, ba, conv_w, conv_w, conv_w, alog_g, dtb_g, norm_w)


def _gdn_step_body(new_ref, buf_ref, cw_ref, z_ref, b_ref, a_ref, alog_ref, dtb_ref, nw_ref, s_ref,
                   o_ref, sout_ref, cout_ref):
    nh = GDN_HEADS
    new = new_ref[0]
    y = new * cw_ref[GDN_CONV - 1]
    for j in range(GDN_CONV - 1):
        y = y + buf_ref[0, j] * cw_ref[j]
        if j > 0:
            cout_ref[0, j - 1] = buf_ref[0, j]
    cout_ref[0, GDN_CONV - 2] = new
    y = y * _sigmoid(y)
    q = _l2norm(y[0:nh]) * GDN_DK ** -0.5
    k = _l2norm(y[nh:2 * nh])
    v = y[2 * nh:3 * nh]
    beta = _sigmoid(b_ref[0])
    eg = jnp.exp(-jnp.exp(alog_ref[...]) * _softplus(a_ref[0] + dtb_ref[...]))
    q_t = q.T
    k_t = k.T
    z = z_ref[0]
    for h in range(nh):
        s_old = s_ref[0, h]
        kcol = k_t[:, h:h + 1]
        egh = eg[h:h + 1, :]
        sk = jnp.sum(kcol * s_old, axis=0, keepdims=True)
        v_new = beta[h:h + 1, :] * (v[h:h + 1, :] - egh * sk)
        s_new = s_old * egh + kcol * v_new
        sout_ref[0, h] = s_new
        o = jnp.sum(q_t[:, h:h + 1] * s_new, axis=0, keepdims=True)
        zh = z[h:h + 1, :]
        on = o * lax.rsqrt(jnp.mean(o * o, -1, keepdims=True) + 1e-6) * nw_ref[...] * (zh * _sigmoid(zh))
        o_ref[0, h:h + 1, :] = on


def _gdn_step(qkv_new, conv_buf, conv_w, z, b_in, a_in, a_log, dt_bias, norm_w, state):
    bs = qkv_new.shape[0]
    nh, nr = GDN_HEADS, 3 * GDN_HEADS
    per_b3 = lambda shape: pl.BlockSpec((1,) + shape, lambda b: (b, 0, 0))
    per_b4 = lambda shape: pl.BlockSpec((1,) + shape, lambda b: (b, 0, 0, 0))
    return pl.pallas_call(
        _gdn_step_body,
        grid=(bs,),
        in_specs=[per_b3((nr, GDN_DK)), per_b4((GDN_CONV - 1, nr, GDN_DK)),
                  pl.BlockSpec((GDN_CONV, nr, GDN_DK), lambda b: (0, 0, 0)),
                  per_b3((nh, GDN_DV)), per_b3((nh, 1)), per_b3((nh, 1)),
                  pl.BlockSpec((nh, 1), lambda b: (0, 0)), pl.BlockSpec((nh, 1), lambda b: (0, 0)),
                  pl.BlockSpec((1, GDN_DV), lambda b: (0, 0)),
                  per_b4((nh, GDN_DK, GDN_DV))],
        out_specs=[per_b3((nh, GDN_DV)), per_b4((nh, GDN_DK, GDN_DV)), per_b4((GDN_CONV - 1, nr, GDN_DK))],
        out_shape=[jax.ShapeDtypeStruct((bs, nh, GDN_DV), F32),
                   jax.ShapeDtypeStruct((bs, nh, GDN_DK, GDN_DV), F32),
                   jax.ShapeDtypeStruct((bs, GDN_CONV - 1, nr, GDN_DK), F32)],
        compiler_params=_cparams(("parallel",)),
    )(qkv_new, conv_buf, conv_w, z, b_in, a_in, a_log, dt_bias, norm_w, state)


def _lane_partner(x, lane, s):
    return jnp.where((lane & s) != 0, pltpu.roll(x, s, 1), pltpu.roll(x, LANES - s, 1))


def _group_reduce(x, lane, op):
    s = 1
    while s < GROUP_SIZE:
        x = op(x, _lane_partner(x, lane, s))
        s *= 2
    return x


def _router_body(hp_ref, rw_ref, rb_ref, idx_ref, w_ref, rank_ref, cnt_ref, carry_s):
    i = pl.program_id(0)

    @pl.when(i == 0)
    def _():
        carry_s[...] = jnp.zeros(carry_s.shape, F32)

    half = rw_ref.shape[0] // 2
    lo, hi = _unpack_bf16_pairs(_load_slabs(hp_ref, ROW_TILE, half // LANES))
    rw = rw_ref[...].astype(BF16)
    scores = _sigmoid(_dot(lo, rw[:half]) + _dot(hi, rw[half:]))
    choice = scores + rb_ref[...]
    tm = scores.shape[0]
    lane = lax.broadcasted_iota(I32, (tm, LANES), 1)
    grp = lane // GROUP_SIZE
    big = jnp.int32(2 * LANES)

    m1 = _group_reduce(choice, lane, jnp.maximum)
    first = _group_reduce(jnp.where(choice == m1, lane, big), lane, jnp.minimum)
    m2 = _group_reduce(jnp.where(lane == first, NEG_INF, choice), lane, jnp.maximum)
    gs = m1 + m2
    beaten = jnp.zeros((tm, LANES), I32)
    for d in range(1, N_GROUPS):
        other = pltpu.roll(gs, d * GROUP_SIZE, 1)
        other_grp = pltpu.roll(grp, d * GROUP_SIZE, 1)
        beats = (other > gs) | ((other == gs) & (other_grp < grp))
        beaten = beaten + beats.astype(I32)
    masked = jnp.where(beaten < TOPK_GROUPS, choice, NEG_INF)

    idx_out = jnp.zeros((tm, LANES), I32)
    w_out = jnp.zeros((tm, LANES), F32)
    onehot = jnp.zeros((tm, LANES), F32)
    sels = []
    for k in range(TOP_K):
        m = jnp.max(masked, axis=-1, keepdims=True)
        idx = jnp.min(jnp.where(masked == m, lane, big), axis=-1, keepdims=True)
        sel = lane == idx
        sels.append(sel)
        wk = jnp.sum(jnp.where(sel, scores, 0.0), axis=-1, keepdims=True)
        idx_out = jnp.where(lane == k, idx, idx_out)
        w_out = jnp.where(lane == k, wk, w_out)
        onehot = jnp.where(sel, 1.0, onehot)
        masked = jnp.where(sel, NEG_INF, masked)
    w_out = w_out / jnp.sum(w_out, axis=-1, keepdims=True) * ROUTED_SCALE

    ri = lax.broadcasted_iota(I32, (tm, tm), 0)
    ci = lax.broadcasted_iota(I32, (tm, tm), 1)
    before = _dot((ri > ci).astype(BF16), onehot.astype(BF16)) + carry_s[...]
    rank_out = jnp.zeros((tm, LANES), F32)
    for k in range(TOP_K):
        rk = jnp.sum(jnp.where(sels[k], before, 0.0), axis=-1, keepdims=True)
        rank_out = jnp.where(lane == k, rk, rank_out)
    carry_s[...] = carry_s[...] + jnp.sum(onehot, axis=0, keepdims=True)

    idx_ref[...] = idx_out
    w_ref[...] = w_out
    rank_ref[...] = rank_out.astype(I32)
    cnt_ref[...] = jnp.broadcast_to(carry_s[...], cnt_ref.shape).astype(I32)


def _router(h_slabs, router_w, router_bias):
    d = router_w.shape[0]
    pitch = d // 2 // LANES
    t = h_slabs.shape[0] // pitch
    row = pl.BlockSpec((ROW_TILE, LANES), lambda i: (i, 0))
    return pl.pallas_call(
        _router_body,
        grid=(t // ROW_TILE,),
        in_specs=[pl.BlockSpec((ROW_TILE * pitch, LANES), lambda i: (i, 0)),
                  pl.BlockSpec((d, N_EXPERTS), lambda i: (0, 0)),
                  pl.BlockSpec((1, N_EXPERTS), lambda i: (0, 0))],
        out_specs=[row, row, row, pl.BlockSpec((SUBLANES, LANES), lambda i: (0, 0))],
        out_shape=[jax.ShapeDtypeStruct((t, LANES), I32), jax.ShapeDtypeStruct((t, LANES), F32),
                   jax.ShapeDtypeStruct((t, LANES), I32), jax.ShapeDtypeStruct((SUBLANES, LANES), I32)],
        scratch_shapes=[pltpu.VMEM((1, LANES), F32)],
        compiler_params=_cparams(("arbitrary",)),
    )(h_slabs, router_w, router_bias)


def _gather_body(idx_ref, tab_ref, out_ref, sem, *, rows, pitch):
    base = pl.program_id(0) * rows

    def slab_copy(src, dst):
        return pltpu.make_async_copy(tab_ref.at[pl.ds(src * pitch, pitch)], out_ref.at[pl.ds(dst * pitch, pitch)], sem)

    def issue(r, carry):
        slab_copy(idx_ref[0, 0, r], base + r).start()
        return carry

    def drain(r, carry):
        slab_copy(0, base).wait()
        return carry

    lax.fori_loop(0, rows, issue, 0)
    lax.fori_loop(0, rows, drain, 0)


def _slab_gather(table, idx, pitch):
    n = idx.shape[0]
    rows = GATHER_ROWS
    assert n % rows == 0
    return pl.pallas_call(
        functools.partial(_gather_body, rows=rows, pitch=pitch),
        grid=(n // rows,),
        in_specs=[pl.BlockSpec((1, 1, rows), lambda i: (i, 0, 0), memory_space=pltpu.SMEM),
                  pl.BlockSpec(memory_space=pl.ANY)],
        out_specs=pl.BlockSpec(memory_space=pl.ANY),
        out_shape=jax.ShapeDtypeStruct((n * pitch, LANES), table.dtype),
        scratch_shapes=[pltpu.SemaphoreType.DMA(())],
        compiler_params=_cparams(("arbitrary",)),
    )(idx.reshape(n // rows, 1, rows), table)


def _new_expert(be_ref, i):
    return (i == 0) | (be_ref[i] != be_ref[jnp.maximum(i - 1, 0)])


def _ffn_up_body(be_ref, bv_ref, x_ref, wg_ref, wu_ref, o_ref, wg_s, wu_s):
    i = pl.program_id(1)

    @pl.when(_new_expert(be_ref, i))
    def _():
        wg_s[...] = wg_ref[...].astype(BF16)
        wu_s[...] = wu_ref[...].astype(BF16)

    @pl.when(bv_ref[i] > 0)
    def _():
        half = wg_s.shape[0] // 2
        lo, hi = _unpack_bf16_pairs(_load_slabs(x_ref, o_ref.shape[0], half // LANES))
        g = _dot(lo, wg_s[:half]) + _dot(hi, wg_s[half:])
        u = _dot(lo, wu_s[:half]) + _dot(hi, wu_s[half:])
        o_ref[...] = (g * _sigmoid(g) * u).astype(o_ref.dtype)

    @pl.when(bv_ref[i] == 0)
    def _():
        o_ref[...] = jnp.zeros(o_ref.shape, o_ref.dtype)


def _ffn_down_body(be_ref, bv_ref, a_ref, wd_ref, o_ref, *, slab_out, k_chunk):
    i = pl.program_id(1)

    @pl.when(bv_ref[i] > 0)
    def _():
        a = a_ref[...]
        acc = None
        for c0 in range(0, a.shape[1], k_chunk):
            part = _dot(a[:, c0:c0 + k_chunk], wd_ref[c0:c0 + k_chunk, :].astype(BF16))
            acc = part if acc is None else acc + part
        if slab_out:
            _store_slabs(o_ref, acc)
        else:
            o_ref[...] = acc

    @pl.when(bv_ref[i] == 0)
    def _():
        o_ref[...] = jnp.zeros(o_ref.shape, o_ref.dtype)


def _expert_ffn(x_slabs, block_e, block_valid, block_row, w_gate, w_up, w_down, layer, tm, tf, tn, slab_out):
    d = w_gate.shape[-2]
    ff = w_gate.shape[-1]
    pitch = d // 2 // LANES
    n_rows = x_slabs.shape[0] // pitch
    nb = block_e.shape[0]
    assert not slab_out or tn == d
    up = pl.pallas_call(
        _ffn_up_body,
        grid_spec=pltpu.PrefetchScalarGridSpec(
            num_scalar_prefetch=2,
            grid=(ff // tf, nb),
            in_specs=[pl.BlockSpec((tm * pitch, LANES), lambda j, i, be, bv: (bv[nb + i], 0)),
                      pl.BlockSpec((None, None, d, tf), lambda j, i, be, bv: (layer, be[i], 0, j)),
                      pl.BlockSpec((None, None, d, tf), lambda j, i, be, bv: (layer, be[i], 0, j))],
            out_specs=pl.BlockSpec((tm, tf), lambda j, i, be, bv: (i, j)),
            scratch_shapes=[pltpu.VMEM((d, tf), BF16), pltpu.VMEM((d, tf), BF16)],
        ),
        out_shape=jax.ShapeDtypeStruct((n_rows, ff), BF16),
        compiler_params=_cparams(("arbitrary", "arbitrary")),
    )
    bv = jnp.concatenate([block_valid, block_row])
    act = up(block_e, bv, x_slabs, w_gate, w_up)
    if slab_out:
        out_spec = pl.BlockSpec((tm * (d // LANES), LANES), lambda j, i, be, bv: (i, 0))
        out_shape = jax.ShapeDtypeStruct((n_rows * (d // LANES), LANES), F32)
    else:
        out_spec = pl.BlockSpec((tm, tn), lambda j, i, be, bv: (i, j))
        out_shape = jax.ShapeDtypeStruct((n_rows, d), F32)
    down = pl.pallas_call(
        functools.partial(_ffn_down_body, slab_out=slab_out, k_chunk=min(ff, 256)),
        grid_spec=pltpu.PrefetchScalarGridSpec(
            num_scalar_prefetch=2,
            grid=(d // tn, nb),
            in_specs=[pl.BlockSpec((tm, ff), lambda j, i, be, bv: (bv[nb + i], 0)),
                      pl.BlockSpec((None, None, ff, tn), lambda j, i, be, bv: (layer, be[i], 0, j))],
            out_specs=out_spec,
        ),
        out_shape=out_shape,
        compiler_params=_cparams(("arbitrary", "arbitrary")),
    )
    return down(block_e, bv, act, w_down)


def _moe(h_slabs, p, layer, t):
    d = p["router_w"].shape[0]
    top_idx, top_w, rank, counts = _router(h_slabs, p["router_w"], p["router_bias"])
    tm = MOE_ROW_BLOCK
    counts = counts[0]
    padded = (counts + tm - 1) // tm * tm
    pad_end = jnp.cumsum(padded)
    pad_start = pad_end - padded
    pos = (pad_start[top_idx[:, :TOP_K]] + rank[:, :TOP_K]).reshape(-1)
    n_blocks = (t * TOP_K) // tm + N_EXPERTS
    n_slots = n_blocks * tm
    slot_tok = jnp.zeros((n_slots,), I32).at[pos].set(jnp.repeat(jnp.arange(t, dtype=I32), TOP_K))
    starts = jnp.arange(n_blocks, dtype=I32) * tm
    n_used = pad_end[-1] // tm
    valid = (starts < pad_end[-1]).astype(I32)
    blk = jnp.minimum(jnp.arange(n_blocks, dtype=I32), n_used - 1)
    block_e = jnp.minimum(jnp.searchsorted(pad_end, blk * tm, side="right"), N_EXPERTS - 1).astype(I32)

    x_sorted = _slab_gather(h_slabs, slot_tok, d // 2 // LANES)
    y_sorted = _expert_ffn(x_sorted, block_e, valid, blk, p["exp_gate"], p["exp_up"], p["exp_down"], layer,
                           tm=tm, tf=256, tn=d, slab_out=True)
    y_tok = _slab_gather(y_sorted, pos, d // LANES)

    stm = _pick_tm(t, 1024)
    nsb = t // stm
    one = jnp.ones((nsb,), I32)
    sblk = jnp.arange(nsb, dtype=I32)
    lead = lambda w: w.reshape((w.shape[0], 1) + w.shape[1:])
    shared = _expert_ffn(h_slabs, 0 * one, one, sblk, lead(p["sh_gate"]), lead(p["sh_up"]), lead(p["sh_down"]),
                         layer, tm=stm, tf=256, tn=1024, slab_out=False)
    return y_tok, top_w, shared


def kernel(x_prompt, x_sample, cache_k, cache_v, state_gdn, state_conv, page_table, c_prompt, c_sample, rel_bias, ada_w, ada_b, w_in, lam_q1, lam_k1, lam_q2, lam_k2, subln_w, conv_w, a_log, dt_bias, gdn_norm_w, w_branch, w_o, ln1_g, ln1_b, router_w, router_bias, exp_gate, exp_up, exp_down, sh_gate, sh_up, sh_down, ln2_g, ln2_b):
    bsz, seq, d = x_prompt.shape
    bs = x_sample.shape[0]
    depth = ada_w.shape[0]
    tp = bsz * seq
    t = tp + bs
    n_pages = page_table.shape[1]
    page = cache_k.shape[2]
    past = n_pages * page
    alpha = (2 * depth) ** 0.25
    assert bs == ROW_TILE and seq % ATTN_BLOCK == 0 and x_sample.shape[1] == 1

    x = jnp.concatenate([x_prompt.reshape(tp, d), x_sample.reshape(bs, d)], axis=0)
    c_all = jnp.concatenate([c_prompt, c_sample, jnp.zeros((-(bsz + bs) % SUBLANES, d), F32)], axis=0)

    q_w = N_ATTN_HEADS * 2 * ATTN_HD
    kv_w = N_KV_HEADS * 2 * ATTN_HD
    hw = GDN_HEADS * GDN_DK
    gdn0 = q_w + 2 * kv_w
    main_w = gdn0 + 4 * hw
    ba_w = 2 * GDN_HEADS

    blk = ATTN_BLOCK
    rel_tile = jnp.arange(blk)[:, None] - jnp.arange(blk)[None, :]
    rel2 = jnp.stack([rel_tile, rel_tile + blk])
    bias_tiles = jnp.where(rel2[None] >= 0, jnp.transpose(rel_bias[_t5_bucket(rel2)].astype(F32), (3, 0, 1, 2)),
                           NEG_INF)
    bias_far = rel_bias[N_BUCKETS - 1].astype(F32)
    rel_dec = past - jnp.arange(past)
    dec = rel_bias[_t5_bucket(rel_dec)].astype(F32).reshape(n_pages, page, N_KV_HEADS, 1, ATTN_GROUP)
    sbias = jnp.broadcast_to(jnp.transpose(dec, (0, 2, 3, 4, 1)),
                             (n_pages, N_KV_HEADS, 2, ATTN_GROUP, page)).reshape(n_pages, N_KV_HEADS, 2 * ATTN_GROUP, page)
    self_b = rel_bias[0].astype(F32).reshape(N_KV_HEADS, 1, ATTN_GROUP)
    sbias_self = jnp.broadcast_to(jnp.broadcast_to(self_b, (N_KV_HEADS, 2, ATTN_GROUP)).reshape(
        N_KV_HEADS, 2 * ATTN_GROUP, 1), (N_KV_HEADS, 2 * ATTN_GROUP, LANES))

    y = x
    outs = {n: [] for n in ("kp", "vp", "sp", "cp", "ks", "vs", "ss", "cs")}
    for l in range(depth):
        lam_init = 0.8 - 0.6 * math.exp(-0.3 * l)
        p = {"router_w": router_w[l], "router_bias": router_bias[l].reshape(1, -1), "exp_gate": exp_gate,
             "exp_up": exp_up, "exp_down": exp_down, "sh_gate": sh_gate, "sh_up": sh_up, "sh_down": sh_down}

        mod = _mm(c_all, ada_w[l], tm=c_all.shape[0], tn=512, n_out=6 * d, bias=ada_b[l].reshape(1, -1),
                  a_silu=True)
        mod_p = mod[:bsz].reshape(bsz, 1, 6 * d)
        mod_s = mod[bsz:bsz + bs]

        h = _modulate(y, mod_p, mod_s, 1, 0, seq)
        tm_big = _pick_tm(t, 1664)
        proj = _mm(h, w_in[l], tm=tm_big, tn=256, n_out=main_w)
        w_tail = w_in[l][:, main_w:]
        w_ba = jnp.pad(w_tail[:, :ba_w], ((0, 0), (0, LANES - ba_w)))
        ba = _mm(h, w_ba, tm=tm_big, tn=LANES, n_out=LANES)
        gates = _mm(h, w_tail[:, ba_w:], tm=tm_big, tn=256, n_out=2 * d)

        lam_vecs = jnp.stack([lam_q1[l], lam_k1[l], lam_q2[l], lam_k2[l]]).astype(F32)
        sub_w = subln_w[l].reshape(1, -1).astype(F32)

        o_a_p = _flash_attention(proj, bias_tiles, bias_far, lam_vecs, sub_w, bsz, seq, lam_init)
        srow = proj[tp:]
        q_s = srow[:, :q_w].reshape(bs, N_KV_HEADS, ATTN_GROUP, 2, ATTN_HD)
        zeros = jnp.zeros_like(q_s[:, :, :, 0])
        qbd = jnp.stack([jnp.concatenate([q_s[:, :, :, 0], zeros], -1),
                         jnp.concatenate([zeros, q_s[:, :, :, 1]], -1)], axis=2)
        qbd = qbd.reshape(bs, N_KV_HEADS, 2 * ATTN_GROUP, 2 * ATTN_HD)
        k_s = srow[:, q_w:q_w + kv_w]
        v_s = srow[:, q_w + kv_w:gdn0]
        o_a_s = _paged_attention(page_table, qbd, k_s.reshape(bs, N_KV_HEADS, 1, 2 * ATTN_HD),
                                 v_s.reshape(bs, N_KV_HEADS, 1, ATTN_VD), cache_k, cache_v, l, sbias,
                                 sbias_self, lam_vecs, sub_w, lam_init)
        o_a = jnp.concatenate([o_a_p, o_a_s.reshape(bs, -1).astype(BF16)], axis=0)

        n_hg = GDN_HEADS // GDN_HEAD_GROUP
        group_lanes = lambda vec: jnp.pad(vec.astype(F32).reshape(n_hg, 1, GDN_HEAD_GROUP),
                                          ((0, 0), (0, 0), (0, LANES - GDN_HEAD_GROUP)))
        nw = gdn_norm_w[l].reshape(1, -1).astype(F32)
        o_g_p, s_p = _gdn_prompt(proj, ba, conv_w[l], group_lanes(a_log[l]), group_lanes(dt_bias[l]), nw,
                                 bsz, seq, gdn0)
        nr = 3 * GDN_HEADS
        o_g_s, s_s, c_s = _gdn_step(
            srow[:, gdn0:gdn0 + 3 * hw].reshape(bs, nr, GDN_DK),
            state_conv[l].reshape(bs, GDN_CONV - 1, nr, GDN_DK),
            conv_w[l].reshape(GDN_CONV, nr, GDN_DK),
            srow[:, gdn0 + 3 * hw:main_w].reshape(bs, GDN_HEADS, GDN_DV),
            ba[tp:, :GDN_HEADS].reshape(bs, GDN_HEADS, 1), ba[tp:, GDN_HEADS:ba_w].reshape(bs, GDN_HEADS, 1),
            a_log[l].reshape(-1, 1).astype(F32), dt_bias[l].reshape(-1, 1).astype(F32), nw, state_gdn[l])
        o_g = jnp.concatenate([o_g_p, o_g_s.reshape(bs, -1).astype(BF16)], axis=0)

        tm_mid = _pick_tm(t, 1024)
        m_a = _mm(o_a, w_branch[l], tm=tm_mid, tn=512, n_out=d, gate=gates)
        merged = _mm(o_g, w_branch[l], tm=tm_mid, tn=512, n_out=d, w_row_blk=1, gate=gates,
                     gate_col_blk=d // 512, prev=m_a, out_dtype=BF16)
        attn_out = _mm(merged, w_o[l], tm=tm_mid, tn=512, n_out=d)
        x1, h2p = _ln1(y, attn_out, mod_p, mod_s, ln1_g[l].reshape(1, -1), ln1_b[l].reshape(1, -1), seq, alpha)

        y_tok, top_w, shared = _moe(h2p, p, l, t)
        y = _final(x1, y_tok, top_w, shared, mod_p, mod_s, ln2_g[l].reshape(1, -1), ln2_b[l].reshape(1, -1),
                   seq, alpha)

        kcols = proj[:, q_w:q_w + kv_w]
        vcols = proj[:, q_w + kv_w:gdn0]
        outs["kp"].append(kcols[:tp].reshape(bsz, seq, N_KV_HEADS, 2 * ATTN_HD))
        outs["vp"].append(vcols[:tp].reshape(bsz, seq, N_KV_HEADS, ATTN_VD))
        outs["sp"].append(s_p)
        outs["cp"].append(proj[:tp, gdn0:gdn0 + 3 * hw].reshape(bsz, seq, 3 * hw)[:, seq - (GDN_CONV - 1):])
        outs["ks"].append(kcols[tp:].reshape(bs, 1, N_KV_HEADS, 2 * ATTN_HD))
        outs["vs"].append(vcols[tp:].reshape(bs, 1, N_KV_HEADS, ATTN_VD))
        outs["ss"].append(s_s)
        outs["cs"].append(c_s.reshape(bs, GDN_CONV - 1, 3 * hw))

    st = lambda n: jnp.stack(outs[n])
    return (y[:tp].reshape(bsz, seq, d), y[tp:].reshape(bs, 1, d), st("kp"), st("vp"), st("sp"), st("cp"),
            st("ks"), st("vs"), st("ss"), st("cs"))
```

```python
import functools
import math

import jax
import jax.numpy as jnp
from jax import lax
from jax.experimental import pallas as pl
from jax.experimental.pallas import tpu as pltpu

F32 = jnp.float32
BF16 = jnp.bfloat16
U32 = jnp.uint32
I32 = jnp.int32

N_ATTN_HEADS = 16
N_KV_HEADS = 4
ATTN_GROUP = N_ATTN_HEADS // N_KV_HEADS
ATTN_HD = 128
ATTN_VD = 256
N_BUCKETS = 32
MAX_DISTANCE = 128
GDN_HEADS = 32
GDN_DK = 128
GDN_DV = 128
GDN_CONV = 4
GDN_CHUNK = 64
N_EXPERTS = 128
TOP_K = 8
N_GROUPS = 8
GROUP_SIZE = N_EXPERTS // N_GROUPS
TOPK_GROUPS = 4
ROUTED_SCALE = 2.5

LANES = 128
SUBLANES = 8
VMEM_LIMIT = 56 * 1024 * 1024

ROW_TILE = 128
FINAL_TILE = 64
ATTN_BLOCK = 256
GDN_HEAD_GROUP = 8
GDN_STACK = 4
MOE_ROW_BLOCK = 256
GATHER_ROWS = 256
Y_SLAB_PAD = 4

NEG_INF = float("-inf")


def _cparams(sem, vmem=VMEM_LIMIT):
    return pltpu.CompilerParams(dimension_semantics=sem, vmem_limit_bytes=vmem)


def _sigmoid(x):
    return jax.nn.sigmoid(x)


def _dot(a, b):
    return jnp.dot(a, b, preferred_element_type=F32)


def _dot_nt(a, b):
    return lax.dot_general(a, b, (((1,), (1,)), ((), ())), preferred_element_type=F32)


def _dot_tn(a, b):
    return lax.dot_general(a, b, (((0,), (0,)), ((), ())), preferred_element_type=F32)


def _mm_body(*refs, a_silu, has_bias, has_gate, has_prev):
    it = iter(refs)
    a_ref = next(it)
    w_ref = next(it)
    bias_ref = next(it) if has_bias else None
    gate_ref = next(it) if has_gate else None
    prev_ref = next(it) if has_prev else None
    o_ref = next(it)
    a = a_ref[...]
    if a_silu:
        a = a.astype(F32)
        a = a * _sigmoid(a)
    acc = _dot(a.astype(BF16), w_ref[...].astype(BF16))
    if has_bias:
        acc = acc + bias_ref[...]
    if has_gate:
        acc = _sigmoid(gate_ref[...]) * acc
    if has_prev:
        acc = acc + prev_ref[...]
    o_ref[...] = acc.astype(o_ref.dtype)


def _mm(a, w, *, tm, tn, n_out, w_row_blk=0, w_col_blk=0, bias=None, gate=None, gate_col_blk=0,
        prev=None, a_silu=False, out_dtype=F32):
    m, k = a.shape
    assert m % tm == 0 and n_out % tn == 0
    in_specs = [pl.BlockSpec((tm, k), lambda i, j: (i, 0)),
                pl.BlockSpec((k, tn), lambda i, j: (w_row_blk, j + w_col_blk))]
    args = [a, w]
    if bias is not None:
        in_specs.append(pl.BlockSpec((1, tn), lambda i, j: (0, j)))
        args.append(bias)
    if gate is not None:
        in_specs.append(pl.BlockSpec((tm, tn), lambda i, j: (i, j + gate_col_blk)))
        args.append(gate)
    if prev is not None:
        in_specs.append(pl.BlockSpec((tm, tn), lambda i, j: (i, j)))
        args.append(prev)
    body = functools.partial(_mm_body, a_silu=a_silu, has_bias=bias is not None,
                             has_gate=gate is not None, has_prev=prev is not None)
    return pl.pallas_call(
        body,
        grid=(m // tm, n_out // tn),
        in_specs=in_specs,
        out_specs=pl.BlockSpec((tm, tn), lambda i, j: (i, j)),
        out_shape=jax.ShapeDtypeStruct((m, n_out), out_dtype),
        compiler_params=_cparams(("parallel", "arbitrary")),
    )(*args)


def _pick_tm(m, cap):
    best = None
    for t in range(LANES, cap + 1, LANES):
        if m % t == 0:
            best = t
    assert best is not None
    return best


def _row_mod(i, n_prompt_tiles, p_ref, s_ref):
    return jnp.where(i >= n_prompt_tiles, s_ref[...], p_ref[0])


def _mod_specs(d, chunk, tiles_per_seq, n_prompt_tiles, tile=ROW_TILE):
    p_spec = pl.BlockSpec((1, 1, d), lambda i: (jnp.minimum(i, n_prompt_tiles - 1) // tiles_per_seq, 0, chunk))
    s_spec = pl.BlockSpec((tile, d), lambda i: (jnp.maximum(i - n_prompt_tiles, 0), chunk))
    return p_spec, s_spec


def _modulate_body(x_ref, scp_ref, scs_ref, shp_ref, shs_ref, o_ref, *, n_prompt_tiles):
    i = pl.program_id(0)
    sc = _row_mod(i, n_prompt_tiles, scp_ref, scs_ref)
    sh = _row_mod(i, n_prompt_tiles, shp_ref, shs_ref)
    o_ref[...] = (x_ref[...] * (1.0 + sc) + sh).astype(o_ref.dtype)


def _modulate(x, mod_p, mod_s, sc_chunk, sh_chunk, seq):
    t, d = x.shape
    n_prompt_tiles = mod_p.shape[0] * seq // ROW_TILE
    tps = seq // ROW_TILE
    scp, scs = _mod_specs(d, sc_chunk, tps, n_prompt_tiles)
    shp, shs = _mod_specs(d, sh_chunk, tps, n_prompt_tiles)
    return pl.pallas_call(
        functools.partial(_modulate_body, n_prompt_tiles=n_prompt_tiles),
        grid=(t // ROW_TILE,),
        in_specs=[pl.BlockSpec((ROW_TILE, d), lambda i: (i, 0)), scp, scs, shp, shs],
        out_specs=pl.BlockSpec((ROW_TILE, d), lambda i: (i, 0)),
        out_shape=jax.ShapeDtypeStruct((t, d), BF16),
        compiler_params=_cparams(("parallel",)),
    )(x, mod_p, mod_s, mod_p, mod_s)


def _layer_norm_rows(v, g, b):
    mu = jnp.mean(v, -1, keepdims=True)
    var = jnp.mean(jnp.square(v - mu), -1, keepdims=True)
    return (v - mu) * lax.rsqrt(var + 1e-5) * g + b


def _pack_bf16_pairs(h):
    half = h.shape[1] // 2
    bits = lax.bitcast_convert_type(h.astype(BF16).astype(F32), U32)
    return (bits[:, :half] >> 16) | (bits[:, half:] & jnp.uint32(0xFFFF0000))


def _unpack_bf16_pairs(p):
    lo = lax.bitcast_convert_type(p << 16, F32).astype(BF16)
    hi = lax.bitcast_convert_type(p & jnp.uint32(0xFFFF0000), F32).astype(BF16)
    return lo, hi


def _store_slabs(o_ref, v, pitch=None):
    rows, n = v.shape[0], v.shape[1] // LANES
    pitch = n if pitch is None else pitch
    for s in range(n):
        o_ref[pl.ds(s, rows, stride=pitch), :] = v[:, s * LANES:(s + 1) * LANES]
    for s in range(n, pitch):
        o_ref[pl.ds(s, rows, stride=pitch), :] = jnp.zeros((rows, LANES), v.dtype)


def _load_slabs(ref, rows, n, pitch=None, base=0):
    pitch = n if pitch is None else pitch
    return jnp.concatenate([ref[pl.ds(base + s, rows, stride=pitch), :] for s in range(n)], axis=1)


def _ln1_body(x_ref, y_ref, gp_ref, gs_ref, scp_ref, scs_ref, shp_ref, shs_ref, lg_ref, lb_ref,
              x1_ref, hp_ref, *, n_prompt_tiles, alpha):
    i = pl.program_id(0)
    g1 = _row_mod(i, n_prompt_tiles, gp_ref, gs_ref)
    sc = _row_mod(i, n_prompt_tiles, scp_ref, scs_ref)
    sh = _row_mod(i, n_prompt_tiles, shp_ref, shs_ref)
    x1 = _layer_norm_rows(alpha * x_ref[...] + g1 * y_ref[...], lg_ref[...], lb_ref[...])
    x1_ref[...] = x1
    _store_slabs(hp_ref, _pack_bf16_pairs(x1 * (1.0 + sc) + sh))


def _ln1(x, y, mod_p, mod_s, ln_g, ln_b, seq, alpha):
    t, d = x.shape
    n_prompt_tiles = mod_p.shape[0] * seq // ROW_TILE
    tps = seq // ROW_TILE
    gp, gs = _mod_specs(d, 2, tps, n_prompt_tiles)
    shp, shs = _mod_specs(d, 3, tps, n_prompt_tiles)
    scp, scs = _mod_specs(d, 4, tps, n_prompt_tiles)
    row = pl.BlockSpec((ROW_TILE, d), lambda i: (i, 0))
    vec = pl.BlockSpec((1, d), lambda i: (0, 0))
    return pl.pallas_call(
        functools.partial(_ln1_body, n_prompt_tiles=n_prompt_tiles, alpha=alpha),
        grid=(t // ROW_TILE,),
        in_specs=[row, row, gp, gs, scp, scs, shp, shs, vec, vec],
        out_specs=[row, pl.BlockSpec((ROW_TILE * (d // 2 // LANES), LANES), lambda i: (i, 0))],
        out_shape=[jax.ShapeDtypeStruct((t, d), F32), jax.ShapeDtypeStruct((t * (d // 2 // LANES), LANES), U32)],
        compiler_params=_cparams(("parallel",)),
    )(x, y, mod_p, mod_s, mod_p, mod_s, mod_p, mod_s, ln_g, ln_b)


def _final_body(x_ref, yt_ref, tw_ref, sh_ref, gp_ref, gs_ref, lg_ref, lb_ref, o_ref, *,
                n_prompt_tiles, alpha, d):
    i = pl.program_id(0)
    g2 = _row_mod(i, n_prompt_tiles, gp_ref, gs_ref)
    f = sh_ref[...]
    tw = tw_ref[...]
    rows, n = f.shape[0], d // LANES
    for k in range(TOP_K):
        f = f + _load_slabs(yt_ref, rows, n, pitch=TOP_K * n, base=k * n) * tw[:, k:k + 1]
    o_ref[...] = _layer_norm_rows(alpha * x_ref[...] + g2 * f, lg_ref[...], lb_ref[...])


def _final(x1, y_tok, top_w, shared, mod_p, mod_s, ln_g, ln_b, seq, alpha):
    t, d = x1.shape
    tile = FINAL_TILE
    n_prompt_tiles = mod_p.shape[0] * seq // tile
    gp, gs = _mod_specs(d, 5, seq // tile, n_prompt_tiles, tile)
    row = pl.BlockSpec((tile, d), lambda i: (i, 0))
    vec = pl.BlockSpec((1, d), lambda i: (0, 0))
    return pl.pallas_call(
        functools.partial(_final_body, n_prompt_tiles=n_prompt_tiles, alpha=alpha, d=d),
        grid=(t // tile,),
        in_specs=[row, pl.BlockSpec((tile * TOP_K * (d // LANES), LANES), lambda i: (i, 0)),
                  pl.BlockSpec((tile, LANES), lambda i: (i, 0)), row, gp, gs, vec, vec],
        out_specs=row,
        out_shape=jax.ShapeDtypeStruct((t, d), F32),
        compiler_params=_cparams(("parallel",)),
    )(x1, y_tok, top_w, shared, mod_p, mod_s, ln_g, ln_b)


def _t5_bucket(rel):
    n = jnp.maximum(rel, 0)
    max_exact = N_BUCKETS // 2
    nf = jnp.maximum(n, 1).astype(F32)
    large = max_exact + (jnp.log(nf / max_exact) / math.log(MAX_DISTANCE / max_exact)
                         * (N_BUCKETS - max_exact)).astype(I32)
    return jnp.where(n < max_exact, n, jnp.minimum(large, N_BUCKETS - 1))


def _lambda(lam_ref, lam_init):
    lam = lam_ref[...]
    s1 = jnp.sum(lam[0:1] * lam[1:2], axis=-1, keepdims=True)
    s2 = jnp.sum(lam[2:3] * lam[3:4], axis=-1, keepdims=True)
    return jnp.exp(s1) - jnp.exp(s2) + lam_init


def _sub_norm(o, w, lam_init):
    return o * lax.rsqrt(jnp.mean(o * o, -1, keepdims=True) + 1e-5) * w * (1.0 - lam_init)


def _online_update(idx, s, v, m_s, l_s, acc_s):
    m_old = m_s[idx]
    m_new = jnp.maximum(m_old, jnp.max(s, axis=-1, keepdims=True))
    p = jnp.exp(s - m_new)
    corr = jnp.exp(m_old - m_new)
    l_s[idx] = corr * l_s[idx] + jnp.sum(p, axis=-1, keepdims=True)
    acc_s[idx] = corr * acc_s[idx] + _dot(p.astype(BF16), v)
    m_s[idx] = m_new


def _flash_body(far_ref, q_ref, k_ref, v_ref, bt_ref, lam_ref, sub_ref, o_ref, m_s, l_s, acc_s, *,
                scale, lam_init):
    kv = pl.program_id(1)
    qi = pl.program_id(2)
    ki = pl.program_id(3)
    hd, vd, grp = ATTN_HD, ATTN_VD, ATTN_GROUP

    @pl.when(ki == 0)
    def _():
        m_s[...] = jnp.full(m_s.shape, NEG_INF, F32)
        l_s[...] = jnp.zeros(l_s.shape, F32)
        acc_s[...] = jnp.zeros(acc_s.shape, F32)

    def process(get_bias):
        k = k_ref[...].astype(BF16)
        v = v_ref[...].astype(BF16)
        for g in range(grp):
            bias = get_bias(g)
            for j in range(2):
                c0 = g * 2 * hd + j * hd
                q = q_ref[:, c0:c0 + hd].astype(BF16)
                s = _dot_nt(q, k[:, j * hd:(j + 1) * hd]) * scale + bias
                _online_update(g * 2 + j, s, v, m_s, l_s, acc_s)

    @pl.when(ki < qi - 1)
    def _():
        process(lambda g: far_ref[kv * grp + g])

    @pl.when(ki == qi - 1)
    def _():
        process(lambda g: bt_ref[g, 1])

    @pl.when(ki == qi)
    def _():
        process(lambda g: bt_ref[g, 0])
        lam = _lambda(lam_ref, lam_init)
        for g in range(grp):
            o1 = acc_s[g * 2] / l_s[g * 2]
            o2 = acc_s[g * 2 + 1] / l_s[g * 2 + 1]
            o = _sub_norm(o1 - lam * o2, sub_ref[...], lam_init)
            o_ref[:, g * vd:(g + 1) * vd] = o.astype(o_ref.dtype)


def _flash_attention(proj, bias_tiles, bias_far, lam_vecs, subln_w, bsz, seq, lam_init):
    blk = ATTN_BLOCK
    nblk = seq // blk
    grp, hd, vd = ATTN_GROUP, ATTN_HD, ATTN_VD
    qw = grp * 2 * hd
    k_col0 = N_KV_HEADS * qw // (2 * hd)
    v_col0 = k_col0 + N_KV_HEADS
    grid_spec = pltpu.PrefetchScalarGridSpec(
        num_scalar_prefetch=0,
        grid=(bsz, N_KV_HEADS, nblk, nblk),
        in_specs=[
            pl.BlockSpec(memory_space=pltpu.SMEM),
            pl.BlockSpec((blk, qw), lambda b, kv, qi, ki: (b * nblk + qi, kv)),
            pl.BlockSpec((blk, 2 * hd), lambda b, kv, qi, ki: (b * nblk + jnp.minimum(ki, qi), k_col0 + kv)),
            pl.BlockSpec((blk, vd), lambda b, kv, qi, ki: (b * nblk + jnp.minimum(ki, qi), v_col0 + kv)),
            pl.BlockSpec((grp, 2, blk, blk), lambda b, kv, qi, ki: (kv, 0, 0, 0)),
            pl.BlockSpec((4, hd), lambda b, kv, qi, ki: (0, 0)),
            pl.BlockSpec((1, vd), lambda b, kv, qi, ki: (0, 0)),
        ],
        out_specs=pl.BlockSpec((blk, grp * vd), lambda b, kv, qi, ki: (b * nblk + qi, kv)),
        scratch_shapes=[pltpu.VMEM((2 * grp, blk, 1), F32), pltpu.VMEM((2 * grp, blk, 1), F32),
                        pltpu.VMEM((2 * grp, blk, vd), F32)],
    )
    return pl.pallas_call(
        functools.partial(_flash_body, scale=hd ** -0.5, lam_init=lam_init),
        grid_spec=grid_spec,
        out_shape=jax.ShapeDtypeStruct((bsz * seq, N_ATTN_HEADS * vd), BF16),
        compiler_params=_cparams(("parallel", "parallel", "parallel", "arbitrary")),
    )(bias_far, proj, proj, proj, bias_tiles, lam_vecs, subln_w)


def _paged_body(pt_ref, qbd_ref, kn_ref, vn_ref, sb_ref, sbs_ref, lam_ref, sub_ref, *rest, n_pages, scale,
                lam_init):
    k_refs = rest[:n_pages]
    v_refs = rest[n_pages:2 * n_pages]
    o_ref = rest[2 * n_pages]
    grp, vd, hd, nkv = ATTN_GROUP, ATTN_VD, ATTN_HD, N_KV_HEADS
    rows = nkv * 2 * grp
    q = qbd_ref[0].astype(BF16)
    q_lo, q_hi = q[:, :hd], q[:, hd:]
    qf = q.astype(F32)
    kn = kn_ref[0].astype(BF16).astype(F32)
    vn = vn_ref[0].astype(BF16).astype(F32)
    kn_rows = jnp.concatenate([jnp.broadcast_to(kn[kv:kv + 1], (2 * grp, 2 * hd)) for kv in range(nkv)], axis=0)
    vn_rows = jnp.concatenate([jnp.broadcast_to(vn[kv:kv + 1], (2 * grp, vd)) for kv in range(nkv)], axis=0)
    s_self = jnp.sum(qf * kn_rows, axis=-1, keepdims=True) * scale + sbs_ref[:, 0:1]

    def flat(ref, c):
        blk = ref[:, :, c * hd:(c + 1) * hd]
        return blk.reshape(blk.shape[0] * blk.shape[1], hd).astype(BF16)

    s_pages = []
    for i in range(n_pages):
        s = _dot_nt(q_lo, flat(k_refs[i], 0)) + _dot_nt(q_hi, flat(k_refs[i], 1))
        s_pages.append(s * scale + sb_ref[i])
    m_elem = s_pages[0]
    for s in s_pages[1:]:
        m_elem = jnp.maximum(m_elem, s)
    m = jnp.maximum(jnp.max(m_elem, axis=-1, keepdims=True), s_self)
    p_self = jnp.exp(s_self - m)
    acc_lo = p_self * vn_rows[:, :hd]
    acc_hi = p_self * vn_rows[:, hd:]
    l_elem = None
    for i in range(n_pages):
        p = jnp.exp(s_pages[i] - m)
        l_elem = p if l_elem is None else l_elem + p
        pb = p.astype(BF16)
        acc_lo = acc_lo + _dot(pb, flat(v_refs[i], 0))
        acc_hi = acc_hi + _dot(pb, flat(v_refs[i], 1))
    l = p_self + jnp.sum(l_elem, axis=-1, keepdims=True)
    n = jnp.concatenate([acc_lo, acc_hi], axis=1) / l
    lam = _lambda(lam_ref, lam_init)
    for kv in range(nkv):
        r0 = kv * 2 * grp
        o = _sub_norm(n[r0:r0 + grp] - lam * n[r0 + grp:r0 + 2 * grp], sub_ref[...], lam_init)
        for g in range(grp):
            c0 = (kv * grp + g) * vd
            o_ref[0, :, c0:c0 + vd] = o[g:g + 1]


def _paged_attention(page_table, qbd, k_new, v_new, cache_k, cache_v, layer, sbias, sbias_self, lam_vecs,
                     subln_w, lam_init):
    bs, n_pages = page_table.shape
    page = cache_k.shape[2]
    kw = 2 * ATTN_HD
    grp, vd = ATTN_GROUP, ATTN_VD

    rows = N_KV_HEADS * 2 * grp

    def page_spec(width, i):
        return pl.BlockSpec((None, None, page, N_KV_HEADS, width), lambda b, pt: (layer, pt[b, i], 0, 0, 0))

    grid_spec = pltpu.PrefetchScalarGridSpec(
        num_scalar_prefetch=1,
        grid=(bs,),
        in_specs=[
            pl.BlockSpec((1, rows, kw), lambda b, pt: (b, 0, 0)),
            pl.BlockSpec((1, N_KV_HEADS, kw), lambda b, pt: (b, 0, 0)),
            pl.BlockSpec((1, N_KV_HEADS, vd), lambda b, pt: (b, 0, 0)),
            pl.BlockSpec((n_pages, rows, page * N_KV_HEADS), lambda b, pt: (0, 0, 0)),
            pl.BlockSpec((rows, LANES), lambda b, pt: (0, 0)),
            pl.BlockSpec((4, ATTN_HD), lambda b, pt: (0, 0)),
            pl.BlockSpec((1, vd), lambda b, pt: (0, 0)),
        ] + [page_spec(kw, i) for i in range(n_pages)] + [page_spec(vd, i) for i in range(n_pages)],
        out_specs=pl.BlockSpec((1, 1, N_ATTN_HEADS * vd), lambda b, pt: (b, 0, 0)),
    )
    return pl.pallas_call(
        functools.partial(_paged_body, n_pages=n_pages, scale=ATTN_HD ** -0.5, lam_init=lam_init),
        grid_spec=grid_spec,
        out_shape=jax.ShapeDtypeStruct((bs, 1, N_ATTN_HEADS * vd), F32),
        compiler_params=_cparams(("parallel",)),
    )(page_table, qbd, k_new, v_new, sbias, sbias_self, lam_vecs, subln_w, *([cache_k] * n_pages),
      *([cache_v] * n_pages))


def _softplus(x):
    return jnp.maximum(x, 0.0) + jnp.log(1.0 + jnp.exp(-jnp.abs(x)))


def _l2norm(x):
    return x * lax.rsqrt(jnp.sum(x * x, -1, keepdims=True) + 1e-6)


def _split_bf16(x):
    hi = x.astype(BF16)
    return hi, (x - hi.astype(F32)).astype(BF16)


def _unit_lower_inverse(a, nilpotency):
    n = a.shape[0]
    eye = (lax.broadcasted_iota(I32, (n, n), 0) == lax.broadcasted_iota(I32, (n, n), 1)).astype(F32)
    y = -a
    r = eye + y
    span = 2
    while span < nilpotency:
        y_hi, y_lo = _split_bf16(y)
        y = _dot(y_hi, y_hi) + (_dot(y_hi, y_lo) + _dot(y_lo, y_hi))
        y_hi, y_lo = _split_bf16(y)
        r_hi, r_lo = _split_bf16(r)
        r = r + (_dot(r_hi, y_hi) + (_dot(r_hi, y_lo) + _dot(r_lo, y_hi)))
        span *= 2
    return r


def _gdn_body(q_ref, k_ref, v_ref, qp_ref, kp_ref, vp_ref, z_ref, ba_ref, cwq_ref, cwk_ref, cwv_ref,
              alog_ref, dtb_ref, nw_ref, o_ref, sout_ref, s_s, *, hg_size, chunk):
    hg = pl.program_id(1)
    c = pl.program_id(2)
    first = c == 0
    dk, dv = GDN_DK, GDN_DV

    @pl.when(first)
    def _():
        s_s[...] = jnp.zeros(s_s.shape, F32)

    def conv_silu(cur_ref, prev_ref, w_ref):
        cur = cur_ref[...]
        prev = jnp.where(first, 0.0, prev_ref[...])
        ext = jnp.concatenate([prev, cur], axis=0)
        y = cur * w_ref[GDN_CONV - 1:GDN_CONV, :]
        for d in range(1, GDN_CONV):
            shifted = pltpu.roll(ext, d, 0)[SUBLANES:SUBLANES + chunk]
            y = y + shifted * w_ref[GDN_CONV - 1 - d:GDN_CONV - d, :]
        return y * _sigmoid(y)

    qc = conv_silu(q_ref, qp_ref, cwq_ref)
    kc = conv_silu(k_ref, kp_ref, cwk_ref)
    vc = conv_silu(v_ref, vp_ref, cwv_ref)

    raw = ba_ref[...]
    off = hg * hg_size
    b_raw = pltpu.roll(raw, (LANES - off) % LANES, 1)
    a_raw = pltpu.roll(raw, (2 * LANES - GDN_HEADS - off) % LANES, 1)
    beta_all = _sigmoid(b_raw)
    g_all = -jnp.exp(alog_ref[0]) * _softplus(a_raw + dtb_ref[0])
    rows = lax.broadcasted_iota(I32, g_all.shape, 0)
    gc_all = g_all
    span = 1
    while span < chunk:
        gc_all = gc_all + jnp.where(rows >= span, pltpu.roll(gc_all, span, 0), 0.0)
        span *= 2
    gc_t = gc_all.T
    eg_all = jnp.exp(gc_all)

    nst = GDN_STACK
    rows_g = nst * chunk
    ri = lax.broadcasted_iota(I32, (rows_g, rows_g), 0)
    ci = lax.broadcasted_iota(I32, (rows_g, rows_g), 1)
    same_head = (ri // chunk) == (ci // chunk)
    causal = same_head & (ri >= ci)
    strict = same_head & (ri > ci)

    def stack(x_all, width, h0):
        return jnp.concatenate([x_all[:, (h0 + s) * width:(h0 + s + 1) * width] for s in range(nst)], axis=0)

    z_all = z_ref[...]
    for h0 in range(0, hg_size, nst):
        q = _l2norm(stack(qc, dk, h0)) * dk ** -0.5
        k = _l2norm(stack(kc, dk, h0))
        v = stack(vc, dv, h0)
        beta = stack(beta_all, 1, h0)
        gcol = stack(gc_all, 1, h0)
        egc = stack(eg_all, 1, h0)
        grow = jnp.concatenate([gc_t[h0 + s:h0 + s + 1, :] for s in range(nst)], axis=1)
        decay = jnp.where(causal, jnp.exp(jnp.where(causal, gcol - grow, 0.0)), 0.0)
        kb = k * beta
        kbf = k.astype(BF16)
        p = _dot_nt(jnp.concatenate([kb, q], axis=0).astype(BF16), kbf)
        a_mat = jnp.where(strict, p[:rows_g] * decay, 0.0)
        qk = p[rows_g:] * decay
        t_mat = _unit_lower_inverse(a_mat, chunk).astype(BF16)
        uw = _dot(t_mat, jnp.concatenate([v * beta, kb * egc], axis=1).astype(BF16))
        u, w = uw[:, :dv], uw[:, dv:]
        qd = q * egc
        ws, qs = [], []
        for s in range(nst):
            r0, r1 = s * chunk, (s + 1) * chunk
            both = _dot(jnp.concatenate([w[r0:r1], qd[r0:r1]], axis=0).astype(BF16), s_s[h0 + s].astype(BF16))
            ws.append(both[:chunk])
            qs.append(both[chunk:])
        v_new = u - jnp.concatenate(ws, axis=0)
        v_new_bf = v_new.astype(BF16)
        o = jnp.concatenate(qs, axis=0) + _dot(qk.astype(BF16), v_new_bf)
        for s in range(nst):
            r0, r1 = s * chunk, (s + 1) * chunk
            g_last = gcol[r1 - 1:r1, :]
            k_dec = (k[r0:r1] * jnp.exp(g_last - gcol[r0:r1])).astype(BF16)
            s_s[h0 + s] = s_s[h0 + s] * jnp.exp(g_last) + _dot_tn(k_dec, v_new_bf[r0:r1])
        z = stack(z_all, dv, h0)
        on = o * lax.rsqrt(jnp.mean(o * o, -1, keepdims=True) + 1e-6) * nw_ref[...] * (z * _sigmoid(z))
        for s in range(nst):
            o_ref[:, (h0 + s) * dv:(h0 + s + 1) * dv] = on[s * chunk:(s + 1) * chunk].astype(o_ref.dtype)

    @pl.when(c == pl.num_programs(2) - 1)
    def _():
        sout_ref[0] = s_s[...]


def _gdn_prompt(proj, ba, conv_w, alog_g, dtb_g, norm_w, bsz, seq, col0):
    hgs, chunk = GDN_HEAD_GROUP, GDN_CHUNK
    n_hg = GDN_HEADS // hgs
    w = hgs * GDN_DK
    n_chunks = seq // chunk
    hw = GDN_HEADS * GDN_DK
    assert col0 % w == 0 and seq % chunk == 0
    qb, kb, vb, zb = (col0 // w + i * (hw // w) for i in range(4))
    cpt = chunk // SUBLANES

    def cur(base):
        return pl.BlockSpec((chunk, w), lambda b, hg, c: (b * n_chunks + c, base + hg))

    def prev(base):
        return pl.BlockSpec((SUBLANES, w),
                            lambda b, hg, c: (jnp.maximum((b * n_chunks + c) * cpt - 1, 0), base + hg))

    def cw(base):
        return pl.BlockSpec((GDN_CONV, w), lambda b, hg, c: (0, base + hg))

    return pl.pallas_call(
        functools.partial(_gdn_body, hg_size=hgs, chunk=chunk),
        grid=(bsz, n_hg, n_chunks),
        in_specs=[cur(qb), cur(kb), cur(vb), prev(qb), prev(kb), prev(vb), cur(zb),
                  pl.BlockSpec((chunk, LANES), lambda b, hg, c: (b * n_chunks + c, 0)),
                  cw(0), cw(hw // w), cw(2 * hw // w),
                  pl.BlockSpec((1, 1, LANES), lambda b, hg, c: (hg, 0, 0)),
                  pl.BlockSpec((1, 1, LANES), lambda b, hg, c: (hg, 0, 0)),
                  pl.BlockSpec((1, GDN_DV), lambda b, hg, c: (0, 0))],
        out_specs=[pl.BlockSpec((chunk, w), lambda b, hg, c: (b * n_chunks + c, hg)),
                   pl.BlockSpec((1, hgs, GDN_DK, GDN_DV), lambda b, hg, c: (b, hg, 0, 0))],
        out_shape=[jax.ShapeDtypeStruct((bsz * seq, hw), BF16),
                   jax.ShapeDtypeStruct((bsz, GDN_HEADS, GDN_DK, GDN_DV), F32)],
        scratch_shapes=[pltpu.VMEM((hgs, GDN_DK, GDN_DV), F32)],
        compiler_params=_cparams(("parallel", "parallel", "arbitrary")),
    )(proj, proj, proj, proj, proj, proj, proj, ba, conv_w, conv_w, conv_w, alog_g, dtb_g, norm_w)


def _gdn_step_body(new_ref, buf_ref, cw_ref, z_ref, b_ref, a_ref, alog_ref, dtb_ref, nw_ref, s_ref,
                   o_ref, sout_ref, cout_ref):
    nh = GDN_HEADS
    new = new_ref[0]
    y = new * cw_ref[GDN_CONV - 1]
    for j in range(GDN_CONV - 1):
        y = y + buf_ref[0, j] * cw_ref[j]
        if j > 0:
            cout_ref[0, j - 1] = buf_ref[0, j]
    cout_ref[0, GDN_CONV - 2] = new
    y = y * _sigmoid(y)
    q = _l2norm(y[0:nh]) * GDN_DK ** -0.5
    k = _l2norm(y[nh:2 * nh])
    v = y[2 * nh:3 * nh]
    beta = _sigmoid(b_ref[0])
    eg = jnp.exp(-jnp.exp(alog_ref[...]) * _softplus(a_ref[0] + dtb_ref[...]))
    q_t = q.T
    k_t = k.T
    z = z_ref[0]
    for h in range(nh):
        s_old = s_ref[0, h]
        kcol = k_t[:, h:h + 1]
        egh = eg[h:h + 1, :]
        sk = jnp.sum(kcol * s_old, axis=0, keepdims=True)
        v_new = beta[h:h + 1, :] * (v[h:h + 1, :] - egh * sk)
        s_new = s_old * egh + kcol * v_new
        sout_ref[0, h] = s_new
        o = jnp.sum(q_t[:, h:h + 1] * s_new, axis=0, keepdims=True)
        zh = z[h:h + 1, :]
        on = o * lax.rsqrt(jnp.mean(o * o, -1, keepdims=True) + 1e-6) * nw_ref[...] * (zh * _sigmoid(zh))
        o_ref[0, h:h + 1, :] = on


def _gdn_step(qkv_new, conv_buf, conv_w, z, b_in, a_in, a_log, dt_bias, norm_w, state):
    bs = qkv_new.shape[0]
    nh, nr = GDN_HEADS, 3 * GDN_HEADS
    per_b3 = lambda shape: pl.BlockSpec((1,) + shape, lambda b: (b, 0, 0))
    per_b4 = lambda shape: pl.BlockSpec((1,) + shape, lambda b: (b, 0, 0, 0))
    return pl.pallas_call(
        _gdn_step_body,
        grid=(bs,),
        in_specs=[per_b3((nr, GDN_DK)), per_b4((GDN_CONV - 1, nr, GDN_DK)),
                  pl.BlockSpec((GDN_CONV, nr, GDN_DK), lambda b: (0, 0, 0)),
                  per_b3((nh, GDN_DV)), per_b3((nh, 1)), per_b3((nh, 1)),
                  pl.BlockSpec((nh, 1), lambda b: (0, 0)), pl.BlockSpec((nh, 1), lambda b: (0, 0)),
                  pl.BlockSpec((1, GDN_DV), lambda b: (0, 0)),
                  per_b4((nh, GDN_DK, GDN_DV))],
        out_specs=[per_b3((nh, GDN_DV)), per_b4((nh, GDN_DK, GDN_DV)), per_b4((GDN_CONV - 1, nr, GDN_DK))],
        out_shape=[jax.ShapeDtypeStruct((bs, nh, GDN_DV), F32),
                   jax.ShapeDtypeStruct((bs, nh, GDN_DK, GDN_DV), F32),
                   jax.ShapeDtypeStruct((bs, GDN_CONV - 1, nr, GDN_DK), F32)],
        compiler_params=_cparams(("parallel",)),
    )(qkv_new, conv_buf, conv_w, z, b_in, a_in, a_log, dt_bias, norm_w, state)


def _lane_partner(x, lane, s):
    return jnp.where((lane & s) != 0, pltpu.roll(x, s, 1), pltpu.roll(x, LANES - s, 1))


def _group_reduce(x, lane, op):
    s = 1
    while s < GROUP_SIZE:
        x = op(x, _lane_partner(x, lane, s))
        s *= 2
    return x


def _router_body(hp_ref, rw_ref, rb_ref, idx_ref, w_ref, rank_ref, cnt_ref, carry_s):
    i = pl.program_id(0)

    @pl.when(i == 0)
    def _():
        carry_s[...] = jnp.zeros(carry_s.shape, F32)

    half = rw_ref.shape[0] // 2
    lo, hi = _unpack_bf16_pairs(_load_slabs(hp_ref, ROW_TILE, half // LANES))
    rw = rw_ref[...].astype(BF16)
    scores = _sigmoid(_dot(lo, rw[:half]) + _dot(hi, rw[half:]))
    choice = scores + rb_ref[...]
    tm = scores.shape[0]
    lane = lax.broadcasted_iota(I32, (tm, LANES), 1)
    grp = lane // GROUP_SIZE
    big = jnp.int32(2 * LANES)

    m1 = _group_reduce(choice, lane, jnp.maximum)
    first = _group_reduce(jnp.where(choice == m1, lane, big), lane, jnp.minimum)
    m2 = _group_reduce(jnp.where(lane == first, NEG_INF, choice), lane, jnp.maximum)
    gs = m1 + m2
    beaten = jnp.zeros((tm, LANES), I32)
    for d in range(1, N_GROUPS):
        other = pltpu.roll(gs, d * GROUP_SIZE, 1)
        other_grp = pltpu.roll(grp, d * GROUP_SIZE, 1)
        beats = (other > gs) | ((other == gs) & (other_grp < grp))
        beaten = beaten + beats.astype(I32)
    masked = jnp.where(beaten < TOPK_GROUPS, choice, NEG_INF)

    idx_out = jnp.zeros((tm, LANES), I32)
    w_out = jnp.zeros((tm, LANES), F32)
    onehot = jnp.zeros((tm, LANES), F32)
    sels = []
    for k in range(TOP_K):
        m = jnp.max(masked, axis=-1, keepdims=True)
        idx = jnp.min(jnp.where(masked == m, lane, big), axis=-1, keepdims=True)
        sel = lane == idx
        sels.append(sel)
        wk = jnp.sum(jnp.where(sel, scores, 0.0), axis=-1, keepdims=True)
        idx_out = jnp.where(lane == k, idx, idx_out)
        w_out = jnp.where(lane == k, wk, w_out)
        onehot = jnp.where(sel, 1.0, onehot)
        masked = jnp.where(sel, NEG_INF, masked)
    w_out = w_out / jnp.sum(w_out, axis=-1, keepdims=True) * ROUTED_SCALE

    ri = lax.broadcasted_iota(I32, (tm, tm), 0)
    ci = lax.broadcasted_iota(I32, (tm, tm), 1)
    before = _dot((ri > ci).astype(BF16), onehot.astype(BF16)) + carry_s[...]
    rank_out = jnp.zeros((tm, LANES), F32)
    for k in range(TOP_K):
        rk = jnp.sum(jnp.where(sels[k], before, 0.0), axis=-1, keepdims=True)
        rank_out = jnp.where(lane == k, rk, rank_out)
    carry_s[...] = carry_s[...] + jnp.sum(onehot, axis=0, keepdims=True)

    idx_ref[...] = idx_out
    w_ref[...] = w_out
    rank_ref[...] = rank_out.astype(I32)
    cnt_ref[...] = jnp.broadcast_to(carry_s[...], cnt_ref.shape).astype(I32)


def _router(h_slabs, router_w, router_bias):
    d = router_w.shape[0]
    pitch = d // 2 // LANES
    t = h_slabs.shape[0] // pitch
    row = pl.BlockSpec((ROW_TILE, LANES), lambda i: (i, 0))
    return pl.pallas_call(
        _router_body,
        grid=(t // ROW_TILE,),
        in_specs=[pl.BlockSpec((ROW_TILE * pitch, LANES), lambda i: (i, 0)),
                  pl.BlockSpec((d, N_EXPERTS), lambda i: (0, 0)),
                  pl.BlockSpec((1, N_EXPERTS), lambda i: (0, 0))],
        out_specs=[row, row, row, pl.BlockSpec((SUBLANES, LANES), lambda i: (0, 0))],
        out_shape=[jax.ShapeDtypeStruct((t, LANES), I32), jax.ShapeDtypeStruct((t, LANES), F32),
                   jax.ShapeDtypeStruct((t, LANES), I32), jax.ShapeDtypeStruct((SUBLANES, LANES), I32)],
        scratch_shapes=[pltpu.VMEM((1, LANES), F32)],
        compiler_params=_cparams(("arbitrary",)),
    )(h_slabs, router_w, router_bias)


def _gather_body(idx_ref, tab_ref, out_ref, sem, *, rows, n, src_pitch):
    def slab_copy(src, dst):
        return pltpu.make_async_copy(tab_ref.at[pl.ds(src * src_pitch, n)],
                                     out_ref.at[pl.ds(pl.multiple_of(dst * n, n), n)], sem)

    def issue(pair, carry):
        for lane in range(2):
            r = 2 * pair + lane
            slab_copy(idx_ref[0, 0, r], r).start(priority=lane)
        return carry

    def drain(r, carry):
        slab_copy(0, 0).wait()
        return carry

    lax.fori_loop(0, rows // 2, issue, 0)
    lax.fori_loop(0, rows, drain, 0)


def _slab_gather(table, idx, n, src_pitch=None):
    src_pitch = n if src_pitch is None else src_pitch
    count = idx.shape[0]
    rows = GATHER_ROWS
    assert count % rows == 0 and rows % 2 == 0
    return pl.pallas_call(
        functools.partial(_gather_body, rows=rows, n=n, src_pitch=src_pitch),
        grid=(count // rows,),
        in_specs=[pl.BlockSpec((1, 1, rows), lambda i: (i, 0, 0), memory_space=pltpu.SMEM),
                  pl.BlockSpec(memory_space=pl.ANY)],
        out_specs=pl.BlockSpec((rows * n, LANES), lambda i: (i, 0)),
        out_shape=jax.ShapeDtypeStruct((count * n, LANES), table.dtype),
        scratch_shapes=[pltpu.SemaphoreType.DMA(())],
        compiler_params=_cparams(("arbitrary",)),
    )(idx.reshape(count // rows, 1, rows), table)


def _new_expert(be_ref, i):
    return (i == 0) | (be_ref[i] != be_ref[jnp.maximum(i - 1, 0)])


def _ffn_up_body(be_ref, bv_ref, x_ref, wg_ref, wu_ref, o_ref, wg_s, wu_s):
    i = pl.program_id(1)

    @pl.when(_new_expert(be_ref, i))
    def _():
        wg_s[...] = wg_ref[...].astype(BF16)
        wu_s[...] = wu_ref[...].astype(BF16)

    @pl.when(bv_ref[i] > 0)
    def _():
        half = wg_s.shape[0] // 2
        lo, hi = _unpack_bf16_pairs(_load_slabs(x_ref, o_ref.shape[0], half // LANES))
        g = _dot(lo, wg_s[:half]) + _dot(hi, wg_s[half:])
        u = _dot(lo, wu_s[:half]) + _dot(hi, wu_s[half:])
        o_ref[...] = (g * _sigmoid(g) * u).astype(o_ref.dtype)

    @pl.when(bv_ref[i] == 0)
    def _():
        o_ref[...] = jnp.zeros(o_ref.shape, o_ref.dtype)


def _ffn_down_body(be_ref, bv_ref, a_ref, wd_ref, o_ref, *, slab_pitch, k_chunk):
    i = pl.program_id(1)

    @pl.when(bv_ref[i] > 0)
    def _():
        a = a_ref[...]
        acc = None
        for c0 in range(0, a.shape[1], k_chunk):
            part = _dot(a[:, c0:c0 + k_chunk], wd_ref[c0:c0 + k_chunk, :].astype(BF16))
            acc = part if acc is None else acc + part
        if slab_pitch:
            _store_slabs(o_ref, acc, slab_pitch)
        else:
            o_ref[...] = acc

    @pl.when(bv_ref[i] == 0)
    def _():
        o_ref[...] = jnp.zeros(o_ref.shape, o_ref.dtype)


def _expert_ffn(x_slabs, block_e, block_valid, block_row, w_gate, w_up, w_down, layer, tm, tf, tn, slab_pitch=0):
    d = w_gate.shape[-2]
    ff = w_gate.shape[-1]
    pitch = d // 2 // LANES
    n_rows = x_slabs.shape[0] // pitch
    nb = block_e.shape[0]
    assert not slab_pitch or tn == d
    up = pl.pallas_call(
        _ffn_up_body,
        grid_spec=pltpu.PrefetchScalarGridSpec(
            num_scalar_prefetch=2,
            grid=(ff // tf, nb),
            in_specs=[pl.BlockSpec((tm * pitch, LANES), lambda j, i, be, bv: (bv[nb + i], 0)),
                      pl.BlockSpec((None, None, d, tf), lambda j, i, be, bv: (layer, be[i], 0, j)),
                      pl.BlockSpec((None, None, d, tf), lambda j, i, be, bv: (layer, be[i], 0, j))],
            out_specs=pl.BlockSpec((tm, tf), lambda j, i, be, bv: (i, j)),
            scratch_shapes=[pltpu.VMEM((d, tf), BF16), pltpu.VMEM((d, tf), BF16)],
        ),
        out_shape=jax.ShapeDtypeStruct((n_rows, ff), BF16),
        compiler_params=_cparams(("arbitrary", "arbitrary")),
    )
    bv = jnp.concatenate([block_valid, block_row])
    act = up(block_e, bv, x_slabs, w_gate, w_up)
    if slab_pitch:
        out_spec = pl.BlockSpec((tm * slab_pitch, LANES), lambda j, i, be, bv: (i, 0))
        out_shape = jax.ShapeDtypeStruct((n_rows * slab_pitch, LANES), F32)
    else:
        out_spec = pl.BlockSpec((tm, tn), lambda j, i, be, bv: (i, j))
        out_shape = jax.ShapeDtypeStruct((n_rows, d), F32)
    down = pl.pallas_call(
        functools.partial(_ffn_down_body, slab_pitch=slab_pitch, k_chunk=min(ff, 256)),
        grid_spec=pltpu.PrefetchScalarGridSpec(
            num_scalar_prefetch=2,
            grid=(d // tn, nb),
            in_specs=[pl.BlockSpec((tm, ff), lambda j, i, be, bv: (bv[nb + i], 0)),
                      pl.BlockSpec((None, None, ff, tn), lambda j, i, be, bv: (layer, be[i], 0, j))],
            out_specs=out_spec,
        ),
        out_shape=out_shape,
        compiler_params=_cparams(("arbitrary", "arbitrary")),
    )
    return down(block_e, bv, act, w_down)


def _moe(h_slabs, p, layer, t):
    d = p["router_w"].shape[0]
    top_idx, top_w, rank, counts = _router(h_slabs, p["router_w"], p["router_bias"])
    tm = MOE_ROW_BLOCK
    counts = counts[0]
    padded = (counts + tm - 1) // tm * tm
    pad_end = jnp.cumsum(padded)
    pad_start = pad_end - padded
    pos = (pad_start[top_idx[:, :TOP_K]] + rank[:, :TOP_K]).reshape(-1)
    n_blocks = (t * TOP_K) // tm + N_EXPERTS
    n_slots = n_blocks * tm
    slot_tok = jnp.zeros((n_slots,), I32).at[pos].set(jnp.repeat(jnp.arange(t, dtype=I32), TOP_K))
    starts = jnp.arange(n_blocks, dtype=I32) * tm
    n_used = pad_end[-1] // tm
    valid = (starts < pad_end[-1]).astype(I32)
    blk = jnp.minimum(jnp.arange(n_blocks, dtype=I32), n_used - 1)
    block_e = jnp.minimum(jnp.searchsorted(pad_end, blk * tm, side="right"), N_EXPERTS - 1).astype(I32)

    x_sorted = _slab_gather(h_slabs, slot_tok, d // 2 // LANES)
    y_pitch = d // LANES + Y_SLAB_PAD
    y_sorted = _expert_ffn(x_sorted, block_e, valid, blk, p["exp_gate"], p["exp_up"], p["exp_down"], layer,
                           tm=tm, tf=min(512, p["exp_gate"].shape[-1]), tn=d, slab_pitch=y_pitch)
    y_tok = _slab_gather(y_sorted, pos, d // LANES, src_pitch=y_pitch)

    stm = _pick_tm(t, 1024)
    nsb = t // stm
    one = jnp.ones((nsb,), I32)
    sblk = jnp.arange(nsb, dtype=I32)
    lead = lambda w: w.reshape((w.shape[0], 1) + w.shape[1:])
    shared = _expert_ffn(h_slabs, 0 * one, one, sblk, lead(p["sh_gate"]), lead(p["sh_up"]), lead(p["sh_down"]),
                         layer, tm=stm, tf=256, tn=1024)
    return y_tok, top_w, shared


def kernel(x_prompt, x_sample, cache_k, cache_v, state_gdn, state_conv, page_table, c_prompt, c_sample, rel_bias, ada_w, ada_b, w_in, lam_q1, lam_k1, lam_q2, lam_k2, subln_w, conv_w, a_log, dt_bias, gdn_norm_w, w_branch, w_o, ln1_g, ln1_b, router_w, router_bias, exp_gate, exp_up, exp_down, sh_gate, sh_up, sh_down, ln2_g, ln2_b):
    bsz, seq, d = x_prompt.shape
    bs = x_sample.shape[0]
    depth = ada_w.shape[0]
    tp = bsz * seq
    t = tp + bs
    n_pages = page_table.shape[1]
    page = cache_k.shape[2]
    past = n_pages * page
    alpha = (2 * depth) ** 0.25
    assert bs == ROW_TILE and seq % ATTN_BLOCK == 0 and x_sample.shape[1] == 1

    x = jnp.concatenate([x_prompt.reshape(tp, d), x_sample.reshape(bs, d)], axis=0)
    c_all = jnp.concatenate([c_prompt, c_sample, jnp.zeros((-(bsz + bs) % SUBLANES, d), F32)], axis=0)

    q_w = N_ATTN_HEADS * 2 * ATTN_HD
    kv_w = N_KV_HEADS * 2 * ATTN_HD
    hw = GDN_HEADS * GDN_DK
    gdn0 = q_w + 2 * kv_w
    main_w = gdn0 + 4 * hw
    ba_w = 2 * GDN_HEADS

    blk = ATTN_BLOCK
    rel_tile = jnp.arange(blk)[:, None] - jnp.arange(blk)[None, :]
    rel2 = jnp.stack([rel_tile, rel_tile + blk])
    bias_tiles = jnp.where(rel2[None] >= 0, jnp.transpose(rel_bias[_t5_bucket(rel2)].astype(F32), (3, 0, 1, 2)),
                           NEG_INF)
    bias_far = rel_bias[N_BUCKETS - 1].astype(F32)
    rel_dec = past - jnp.arange(past)
    dec = rel_bias[_t5_bucket(rel_dec)].astype(F32).reshape(n_pages, page, N_KV_HEADS, ATTN_GROUP)
    dec = jnp.transpose(dec, (0, 2, 3, 1))[:, :, None, :, :, None]
    same_kv = jnp.eye(N_KV_HEADS, dtype=bool)[None, :, None, None, None, :]
    sbias = jnp.where(same_kv, jnp.broadcast_to(dec, (n_pages, N_KV_HEADS, 2, ATTN_GROUP, page, N_KV_HEADS)),
                      NEG_INF).reshape(n_pages, N_KV_HEADS * 2 * ATTN_GROUP, page * N_KV_HEADS)
    self_b = rel_bias[0].astype(F32).reshape(N_KV_HEADS, 1, ATTN_GROUP)
    sbias_self = jnp.broadcast_to(jnp.broadcast_to(self_b, (N_KV_HEADS, 2, ATTN_GROUP)).reshape(
        N_KV_HEADS * 2 * ATTN_GROUP, 1), (N_KV_HEADS * 2 * ATTN_GROUP, LANES))

    y = x
    outs = {n: [] for n in ("kp", "vp", "sp", "cp", "ks", "vs", "ss", "cs")}
    for l in range(depth):
        lam_init = 0.8 - 0.6 * math.exp(-0.3 * l)
        p = {"router_w": router_w[l], "router_bias": router_bias[l].reshape(1, -1), "exp_gate": exp_gate,
             "exp_up": exp_up, "exp_down": exp_down, "sh_gate": sh_gate, "sh_up": sh_up, "sh_down": sh_down}

        mod = _mm(c_all, ada_w[l], tm=c_all.shape[0], tn=512, n_out=6 * d, bias=ada_b[l].reshape(1, -1),
                  a_silu=True)
        mod_p = mod[:bsz].reshape(bsz, 1, 6 * d)
        mod_s = mod[bsz:bsz + bs]

        h = _modulate(y, mod_p, mod_s, 1, 0, seq)
        tm_big = _pick_tm(t, 1664)
        proj = _mm(h, w_in[l], tm=tm_big, tn=256, n_out=main_w)
        w_tail = w_in[l][:, main_w:]
        w_ba = jnp.pad(w_tail[:, :ba_w], ((0, 0), (0, LANES - ba_w)))
        ba = _mm(h, w_ba, tm=tm_big, tn=LANES, n_out=LANES)
        gates = _mm(h, w_tail[:, ba_w:], tm=tm_big, tn=256, n_out=2 * d)

        lam_vecs = jnp.stack([lam_q1[l], lam_k1[l], lam_q2[l], lam_k2[l]]).astype(F32)
        sub_w = subln_w[l].reshape(1, -1).astype(F32)

        o_a_p = _flash_attention(proj, bias_tiles, bias_far, lam_vecs, sub_w, bsz, seq, lam_init)
        srow = proj[tp:]
        q_s = srow[:, :q_w].reshape(bs, N_KV_HEADS, ATTN_GROUP, 2, ATTN_HD)
        zeros = jnp.zeros_like(q_s[:, :, :, 0])
        qbd = jnp.stack([jnp.concatenate([q_s[:, :, :, 0], zeros], -1),
                         jnp.concatenate([zeros, q_s[:, :, :, 1]], -1)], axis=2)
        qbd = qbd.reshape(bs, N_KV_HEADS * 2 * ATTN_GROUP, 2 * ATTN_HD)
        k_s = srow[:, q_w:q_w + kv_w]
        v_s = srow[:, q_w + kv_w:gdn0]
        o_a_s = _paged_attention(page_table, qbd, k_s.reshape(bs, N_KV_HEADS, 2 * ATTN_HD),
                                 v_s.reshape(bs, N_KV_HEADS, ATTN_VD), cache_k, cache_v, l, sbias,
                                 sbias_self, lam_vecs, sub_w, lam_init)
        o_a = jnp.concatenate([o_a_p, o_a_s.reshape(bs, -1).astype(BF16)], axis=0)

        n_hg = GDN_HEADS // GDN_HEAD_GROUP
        group_lanes = lambda vec: jnp.pad(vec.astype(F32).reshape(n_hg, 1, GDN_HEAD_GROUP),
                                          ((0, 0), (0, 0), (0, LANES - GDN_HEAD_GROUP)))
        nw = gdn_norm_w[l].reshape(1, -1).astype(F32)
        o_g_p, s_p = _gdn_prompt(proj, ba, conv_w[l], group_lanes(a_log[l]), group_lanes(dt_bias[l]), nw,
                                 bsz, seq, gdn0)
        nr = 3 * GDN_HEADS
        o_g_s, s_s, c_s = _gdn_step(
            srow[:, gdn0:gdn0 + 3 * hw].reshape(bs, nr, GDN_DK),
            state_conv[l].reshape(bs, GDN_CONV - 1, nr, GDN_DK),
            conv_w[l].reshape(GDN_CONV, nr, GDN_DK),
            srow[:, gdn0 + 3 * hw:main_w].reshape(bs, GDN_HEADS, GDN_DV),
            ba[tp:, :GDN_HEADS].reshape(bs, GDN_HEADS, 1), ba[tp:, GDN_HEADS:ba_w].reshape(bs, GDN_HEADS, 1),
            a_log[l].reshape(-1, 1).astype(F32), dt_bias[l].reshape(-1, 1).astype(F32), nw, state_gdn[l])
        o_g = jnp.concatenate([o_g_p, o_g_s.reshape(bs, -1).astype(BF16)], axis=0)

        tm_mid = _pick_tm(t, 1024)
        m_a = _mm(o_a, w_branch[l], tm=tm_mid, tn=512, n_out=d, gate=gates)
        merged = _mm(o_g, w_branch[l], tm=tm_mid, tn=512, n_out=d, w_row_blk=1, gate=gates,
                     gate_col_blk=d // 512, prev=m_a, out_dtype=BF16)
        attn_out = _mm(merged, w_o[l], tm=tm_mid, tn=512, n_out=d)
        x1, h2p = _ln1(y, attn_out, mod_p, mod_s, ln1_g[l].reshape(1, -1), ln1_b[l].reshape(1, -1), seq, alpha)

        y_tok, top_w, shared = _moe(h2p, p, l, t)
        y = _final(x1, y_tok, top_w, shared, mod_p, mod_s, ln2_g[l].reshape(1, -1), ln2_b[l].reshape(1, -1),
                   seq, alpha)

        kcols = proj[:, q_w:q_w + kv_w]
        vcols = proj[:, q_w + kv_w:gdn0]
        outs["kp"].append(kcols[:tp].reshape(bsz, seq, N_KV_HEADS, 2 * ATTN_HD))
        outs["vp"].append(vcols[:tp].reshape(bsz, seq, N_KV_HEADS, ATTN_VD))
        outs["sp"].append(s_p)
        outs["cp"].append(proj[:tp, gdn0:gdn0 + 3 * hw].reshape(bsz, seq, 3 * hw)[:, seq - (GDN_CONV - 1):])
        outs["ks"].append(kcols[tp:].reshape(bs, 1, N_KV_HEADS, 2 * ATTN_HD))
        outs["vs"].append(vcols[tp:].reshape(bs, 1, N_KV_HEADS, ATTN_VD))
        outs["ss"].append(s_s)
        outs["cs"].append(c_s.reshape(bs, GDN_CONV - 1, 3 * hw))

    st = lambda n: jnp.stack(outs[n])
    return (y[:tp].reshape(bsz, seq, d), y[tp:].reshape(bs, 1, d), st("kp"), st("vp"), st("sp"), st("cp"),
            st("ks"), st("vs"), st("ss"), st("cs"))
```

```python
import functools
import math

import jax
import jax.numpy as jnp
from jax import lax
from jax.experimental import pallas as pl
from jax.experimental.pallas import tpu as pltpu

F32 = jnp.float32
BF16 = jnp.bfloat16
U32 = jnp.uint32
I32 = jnp.int32

N_ATTN_HEADS = 16
N_KV_HEADS = 4
ATTN_GROUP = N_ATTN_HEADS // N_KV_HEADS
ATTN_HD = 128
ATTN_VD = 256
N_BUCKETS = 32
MAX_DISTANCE = 128
GDN_HEADS = 32
GDN_DK = 128
GDN_DV = 128
GDN_CONV = 4
GDN_CHUNK = 64
N_EXPERTS = 128
TOP_K = 8
N_GROUPS = 8
GROUP_SIZE = N_EXPERTS // N_GROUPS
TOPK_GROUPS = 4
ROUTED_SCALE = 2.5

LANES = 128
SUBLANES = 8
VMEM_LIMIT = 56 * 1024 * 1024

ROW_TILE = 128
FINAL_TILE = 64
ATTN_BLOCK = 256
GDN_HEAD_GROUP = 16
GDN_STACK = 4
MOE_ROW_BLOCK = 640
GATHER_ROWS = 256
Y_SLAB_PAD = 4

NEG_INF = float("-inf")


def _cparams(sem, vmem=VMEM_LIMIT):
    return pltpu.CompilerParams(dimension_semantics=sem, vmem_limit_bytes=vmem)


def _sigmoid(x):
    return jax.nn.sigmoid(x)


def _dot(a, b):
    return jnp.dot(a, b, preferred_element_type=F32)


def _dot_nt(a, b):
    return lax.dot_general(a, b, (((1,), (1,)), ((), ())), preferred_element_type=F32)


def _dot_tn(a, b):
    return lax.dot_general(a, b, (((0,), (0,)), ((), ())), preferred_element_type=F32)


def _mm_body(*refs, a_silu, has_bias, has_gate, has_prev):
    it = iter(refs)
    a_ref = next(it)
    w_ref = next(it)
    bias_ref = next(it) if has_bias else None
    gate_ref = next(it) if has_gate else None
    prev_ref = next(it) if has_prev else None
    o_ref = next(it)
    a = a_ref[...]
    if a_silu:
        a = a.astype(F32)
        a = a * _sigmoid(a)
    acc = _dot(a.astype(BF16), w_ref[...].astype(BF16))
    if has_bias:
        acc = acc + bias_ref[...]
    if has_gate:
        acc = _sigmoid(gate_ref[...]) * acc
    if has_prev:
        acc = acc + prev_ref[...]
    o_ref[...] = acc.astype(o_ref.dtype)


def _mm(a, w, *, tm, tn, n_out, w_row_blk=0, w_col_blk=0, bias=None, gate=None, gate_col_blk=0,
        prev=None, a_silu=False, out_dtype=F32):
    m, k = a.shape
    assert m % tm == 0 and n_out % tn == 0
    in_specs = [pl.BlockSpec((tm, k), lambda i, j: (i, 0)),
                pl.BlockSpec((k, tn), lambda i, j: (w_row_blk, j + w_col_blk))]
    args = [a, w]
    if bias is not None:
        in_specs.append(pl.BlockSpec((1, tn), lambda i, j: (0, j)))
        args.append(bias)
    if gate is not None:
        in_specs.append(pl.BlockSpec((tm, tn), lambda i, j: (i, j + gate_col_blk)))
        args.append(gate)
    if prev is not None:
        in_specs.append(pl.BlockSpec((tm, tn), lambda i, j: (i, j)))
        args.append(prev)
    body = functools.partial(_mm_body, a_silu=a_silu, has_bias=bias is not None,
                             has_gate=gate is not None, has_prev=prev is not None)
    return pl.pallas_call(
        body,
        grid=(m // tm, n_out // tn),
        in_specs=in_specs,
        out_specs=pl.BlockSpec((tm, tn), lambda i, j: (i, j)),
        out_shape=jax.ShapeDtypeStruct((m, n_out), out_dtype),
        compiler_params=_cparams(("parallel", "arbitrary")),
    )(*args)


def _pick_tm(m, cap):
    best = None
    for t in range(LANES, cap + 1, LANES):
        if m % t == 0:
            best = t
    assert best is not None
    return best


def _row_mod(i, n_prompt_tiles, p_ref, s_ref):
    return jnp.where(i >= n_prompt_tiles, s_ref[...], p_ref[0])


def _mod_specs(d, chunk, tiles_per_seq, n_prompt_tiles, tile=ROW_TILE):
    p_spec = pl.BlockSpec((1, 1, d), lambda i: (jnp.minimum(i, n_prompt_tiles - 1) // tiles_per_seq, 0, chunk))
    s_spec = pl.BlockSpec((tile, d), lambda i: (jnp.maximum(i - n_prompt_tiles, 0), chunk))
    return p_spec, s_spec


def _modulate_body(x_ref, scp_ref, scs_ref, shp_ref, shs_ref, o_ref, *, n_prompt_tiles):
    i = pl.program_id(0)
    sc = _row_mod(i, n_prompt_tiles, scp_ref, scs_ref)
    sh = _row_mod(i, n_prompt_tiles, shp_ref, shs_ref)
    o_ref[...] = (x_ref[...] * (1.0 + sc) + sh).astype(o_ref.dtype)


def _modulate(x, mod_p, mod_s, sc_chunk, sh_chunk, seq):
    t, d = x.shape
    n_prompt_tiles = mod_p.shape[0] * seq // ROW_TILE
    tps = seq // ROW_TILE
    scp, scs = _mod_specs(d, sc_chunk, tps, n_prompt_tiles)
    shp, shs = _mod_specs(d, sh_chunk, tps, n_prompt_tiles)
    return pl.pallas_call(
        functools.partial(_modulate_body, n_prompt_tiles=n_prompt_tiles),
        grid=(t // ROW_TILE,),
        in_specs=[pl.BlockSpec((ROW_TILE, d), lambda i: (i, 0)), scp, scs, shp, shs],
        out_specs=pl.BlockSpec((ROW_TILE, d), lambda i: (i, 0)),
        out_shape=jax.ShapeDtypeStruct((t, d), BF16),
        compiler_params=_cparams(("parallel",)),
    )(x, mod_p, mod_s, mod_p, mod_s)


def _layer_norm_rows(v, g, b):
    mu = jnp.mean(v, -1, keepdims=True)
    var = jnp.mean(jnp.square(v - mu), -1, keepdims=True)
    return (v - mu) * lax.rsqrt(var + 1e-5) * g + b


def _pack_bf16_pairs(h):
    half = h.shape[1] // 2
    bits = lax.bitcast_convert_type(h.astype(BF16).astype(F32), U32)
    return (bits[:, :half] >> 16) | (bits[:, half:] & jnp.uint32(0xFFFF0000))


def _unpack_bf16_pairs(p):
    lo = lax.bitcast_convert_type(p << 16, F32).astype(BF16)
    hi = lax.bitcast_convert_type(p & jnp.uint32(0xFFFF0000), F32).astype(BF16)
    return lo, hi


def _store_slabs(o_ref, v, pitch=None):
    rows, n = v.shape[0], v.shape[1] // LANES
    pitch = n if pitch is None else pitch
    for s in range(n):
        o_ref[pl.ds(s, rows, stride=pitch), :] = v[:, s * LANES:(s + 1) * LANES]
    for s in range(n, pitch):
        o_ref[pl.ds(s, rows, stride=pitch), :] = jnp.zeros((rows, LANES), v.dtype)


def _load_slabs(ref, rows, n, pitch=None, base=0):
    pitch = n if pitch is None else pitch
    return jnp.concatenate([ref[pl.ds(base + s, rows, stride=pitch), :] for s in range(n)], axis=1)


def _ln1_body(x_ref, y_ref, gp_ref, gs_ref, scp_ref, scs_ref, shp_ref, shs_ref, lg_ref, lb_ref,
              x1_ref, hp_ref, *, n_prompt_tiles, alpha):
    i = pl.program_id(0)
    g1 = _row_mod(i, n_prompt_tiles, gp_ref, gs_ref)
    sc = _row_mod(i, n_prompt_tiles, scp_ref, scs_ref)
    sh = _row_mod(i, n_prompt_tiles, shp_ref, shs_ref)
    x1 = _layer_norm_rows(alpha * x_ref[...] + g1 * y_ref[...], lg_ref[...], lb_ref[...])
    x1_ref[...] = x1
    _store_slabs(hp_ref, _pack_bf16_pairs(x1 * (1.0 + sc) + sh))


def _ln1(x, y, mod_p, mod_s, ln_g, ln_b, seq, alpha):
    t, d = x.shape
    n_prompt_tiles = mod_p.shape[0] * seq // ROW_TILE
    tps = seq // ROW_TILE
    gp, gs = _mod_specs(d, 2, tps, n_prompt_tiles)
    shp, shs = _mod_specs(d, 3, tps, n_prompt_tiles)
    scp, scs = _mod_specs(d, 4, tps, n_prompt_tiles)
    row = pl.BlockSpec((ROW_TILE, d), lambda i: (i, 0))
    vec = pl.BlockSpec((1, d), lambda i: (0, 0))
    return pl.pallas_call(
        functools.partial(_ln1_body, n_prompt_tiles=n_prompt_tiles, alpha=alpha),
        grid=(t // ROW_TILE,),
        in_specs=[row, row, gp, gs, scp, scs, shp, shs, vec, vec],
        out_specs=[row, pl.BlockSpec((ROW_TILE * (d // 2 // LANES), LANES), lambda i: (i, 0))],
        out_shape=[jax.ShapeDtypeStruct((t, d), F32), jax.ShapeDtypeStruct((t * (d // 2 // LANES), LANES), U32)],
        compiler_params=_cparams(("parallel",)),
    )(x, y, mod_p, mod_s, mod_p, mod_s, mod_p, mod_s, ln_g, ln_b)


def _final_body(x_ref, yt_ref, tw_ref, sh_ref, gp_ref, gs_ref, lg_ref, lb_ref, o_ref, *,
                n_prompt_tiles, alpha, d):
    i = pl.program_id(0)
    g2 = _row_mod(i, n_prompt_tiles, gp_ref, gs_ref)
    f = sh_ref[...]
    tw = tw_ref[...]
    rows, n = f.shape[0], d // LANES
    for k in range(TOP_K):
        f = f + _load_slabs(yt_ref, rows, n, pitch=TOP_K * n, base=k * n) * tw[:, k:k + 1]
    o_ref[...] = _layer_norm_rows(alpha * x_ref[...] + g2 * f, lg_ref[...], lb_ref[...])


def _final(x1, y_tok, top_w, shared, mod_p, mod_s, ln_g, ln_b, seq, alpha):
    t, d = x1.shape
    tile = FINAL_TILE
    n_prompt_tiles = mod_p.shape[0] * seq // tile
    gp, gs = _mod_specs(d, 5, seq // tile, n_prompt_tiles, tile)
    row = pl.BlockSpec((tile, d), lambda i: (i, 0))
    vec = pl.BlockSpec((1, d), lambda i: (0, 0))
    return pl.pallas_call(
        functools.partial(_final_body, n_prompt_tiles=n_prompt_tiles, alpha=alpha, d=d),
        grid=(t // tile,),
        in_specs=[row, pl.BlockSpec((tile * TOP_K * (d // LANES), LANES), lambda i: (i, 0)),
                  pl.BlockSpec((tile, LANES), lambda i: (i, 0)), row, gp, gs, vec, vec],
        out_specs=row,
        out_shape=jax.ShapeDtypeStruct((t, d), F32),
        compiler_params=_cparams(("parallel",)),
    )(x1, y_tok, top_w, shared, mod_p, mod_s, ln_g, ln_b)


def _t5_bucket(rel):
    n = jnp.maximum(rel, 0)
    max_exact = N_BUCKETS // 2
    nf = jnp.maximum(n, 1).astype(F32)
    large = max_exact + (jnp.log(nf / max_exact) / math.log(MAX_DISTANCE / max_exact)
                         * (N_BUCKETS - max_exact)).astype(I32)
    return jnp.where(n < max_exact, n, jnp.minimum(large, N_BUCKETS - 1))


def _toeplitz(a):
    n = (a.shape[-1] + 1) // 2
    zero = jnp.zeros(a.shape[:-1] + (1,), a.dtype)
    u = jnp.concatenate([a[..., n - 1::-1], zero, a[..., :n - 1:-1]], axis=-1)
    skew = jnp.tile(u, n)[..., :n * (2 * n - 1)].reshape(a.shape[:-1] + (n, 2 * n - 1))
    return skew[..., :n]


def _lambda(lam_ref, lam_init):
    lam = lam_ref[...]
    s1 = jnp.sum(lam[0:1] * lam[1:2], axis=-1, keepdims=True)
    s2 = jnp.sum(lam[2:3] * lam[3:4], axis=-1, keepdims=True)
    return jnp.exp(s1) - jnp.exp(s2) + lam_init


def _sub_norm(o, w, lam_init):
    return o * lax.rsqrt(jnp.mean(o * o, -1, keepdims=True) + 1e-5) * w * (1.0 - lam_init)


def _online_update(idx, s, v, m_s, l_s, acc_s):
    m_old = m_s[idx]
    m_new = jnp.maximum(m_old, jnp.max(s, axis=-1, keepdims=True))
    p = jnp.exp(s - m_new)
    corr = jnp.exp(m_old - m_new)
    l_s[idx] = corr * l_s[idx] + jnp.sum(p, axis=-1, keepdims=True)
    acc_s[idx] = corr * acc_s[idx] + _dot(p.astype(BF16), v)
    m_s[idx] = m_new


def _flash_body(far_ref, q_ref, k_ref, v_ref, bt_ref, lam_ref, sub_ref, o_ref, m_s, l_s, acc_s, *,
                scale, lam_init):
    kv = pl.program_id(1)
    qi = pl.program_id(2)
    ki = pl.program_id(3)
    hd, vd, grp = ATTN_HD, ATTN_VD, ATTN_GROUP

    @pl.when(ki == 0)
    def _():
        m_s[...] = jnp.full(m_s.shape, NEG_INF, F32)
        l_s[...] = jnp.zeros(l_s.shape, F32)
        acc_s[...] = jnp.zeros(acc_s.shape, F32)

    def process(get_bias):
        k = k_ref[...].astype(BF16)
        v = v_ref[...].astype(BF16)
        for g in range(grp):
            bias = get_bias(g)
            for j in range(2):
                c0 = g * 2 * hd + j * hd
                q = q_ref[:, c0:c0 + hd].astype(BF16)
                s = _dot_nt(q, k[:, j * hd:(j + 1) * hd]) * scale + bias
                _online_update(g * 2 + j, s, v, m_s, l_s, acc_s)

    @pl.when(ki < qi - 1)
    def _():
        process(lambda g: far_ref[kv * grp + g])

    @pl.when(ki == qi - 1)
    def _():
        process(lambda g: bt_ref[g, 1])

    @pl.when(ki == qi)
    def _():
        process(lambda g: bt_ref[g, 0])
        lam = _lambda(lam_ref, lam_init)
        for g in range(grp):
            o1 = acc_s[g * 2] / l_s[g * 2]
            o2 = acc_s[g * 2 + 1] / l_s[g * 2 + 1]
            o = _sub_norm(o1 - lam * o2, sub_ref[...], lam_init)
            o_ref[:, g * vd:(g + 1) * vd] = o.astype(o_ref.dtype)


def _flash_attention(proj, bias_tiles, bias_far, lam_vecs, subln_w, bsz, seq, lam_init):
    blk = ATTN_BLOCK
    nblk = seq // blk
    grp, hd, vd = ATTN_GROUP, ATTN_HD, ATTN_VD
    qw = grp * 2 * hd
    k_col0 = N_KV_HEADS * qw // (2 * hd)
    v_col0 = k_col0 + N_KV_HEADS
    grid_spec = pltpu.PrefetchScalarGridSpec(
        num_scalar_prefetch=0,
        grid=(bsz, N_KV_HEADS, nblk, nblk),
        in_specs=[
            pl.BlockSpec(memory_space=pltpu.SMEM),
            pl.BlockSpec((blk, qw), lambda b, kv, qi, ki: (b * nblk + qi, kv)),
            pl.BlockSpec((blk, 2 * hd), lambda b, kv, qi, ki: (b * nblk + jnp.minimum(ki, qi), k_col0 + kv)),
            pl.BlockSpec((blk, vd), lambda b, kv, qi, ki: (b * nblk + jnp.minimum(ki, qi), v_col0 + kv)),
            pl.BlockSpec((grp, 2, blk, blk), lambda b, kv, qi, ki: (kv, 0, 0, 0)),
            pl.BlockSpec((4, hd), lambda b, kv, qi, ki: (0, 0)),
            pl.BlockSpec((1, vd), lambda b, kv, qi, ki: (0, 0)),
        ],
        out_specs=pl.BlockSpec((blk, grp * vd), lambda b, kv, qi, ki: (b * nblk + qi, kv)),
        scratch_shapes=[pltpu.VMEM((2 * grp, blk, 1), F32), pltpu.VMEM((2 * grp, blk, 1), F32),
                        pltpu.VMEM((2 * grp, blk, vd), F32)],
    )
    return pl.pallas_call(
        functools.partial(_flash_body, scale=hd ** -0.5, lam_init=lam_init),
        grid_spec=grid_spec,
        out_shape=jax.ShapeDtypeStruct((bsz * seq, N_ATTN_HEADS * vd), BF16),
        compiler_params=_cparams(("parallel", "parallel", "parallel", "arbitrary")),
    )(bias_far, proj, proj, proj, bias_tiles, lam_vecs, subln_w)


def _paged_body(pt_ref, qbd_ref, kn_ref, vn_ref, sb_ref, sbs_ref, lam_ref, sub_ref, *rest, n_pages, scale,
                lam_init):
    k_refs = rest[:n_pages]
    v_refs = rest[n_pages:2 * n_pages]
    o_ref = rest[2 * n_pages]
    grp, vd, hd, nkv = ATTN_GROUP, ATTN_VD, ATTN_HD, N_KV_HEADS
    rows = nkv * 2 * grp
    q = qbd_ref[0].astype(BF16)
    q_lo, q_hi = q[:, :hd], q[:, hd:]
    qf = q.astype(F32)
    kn = kn_ref[0].astype(BF16).astype(F32)
    vn = vn_ref[0].astype(BF16).astype(F32)
    kn_rows = jnp.concatenate([jnp.broadcast_to(kn[kv:kv + 1], (2 * grp, 2 * hd)) for kv in range(nkv)], axis=0)
    vn_rows = jnp.concatenate([jnp.broadcast_to(vn[kv:kv + 1], (2 * grp, vd)) for kv in range(nkv)], axis=0)
    s_self = jnp.sum(qf * kn_rows, axis=-1, keepdims=True) * scale + sbs_ref[:, 0:1]

    def flat(ref, c):
        blk = ref[:, :, c * hd:(c + 1) * hd]
        return blk.reshape(blk.shape[0] * blk.shape[1], hd).astype(BF16)

    s_pages = []
    for i in range(n_pages):
        s = _dot_nt(q_lo, flat(k_refs[i], 0)) + _dot_nt(q_hi, flat(k_refs[i], 1))
        s_pages.append(s * scale + sb_ref[i])
    m_elem = s_pages[0]
    for s in s_pages[1:]:
        m_elem = jnp.maximum(m_elem, s)
    m = jnp.maximum(jnp.max(m_elem, axis=-1, keepdims=True), s_self)
    p_self = jnp.exp(s_self - m)
    acc_lo = p_self * vn_rows[:, :hd]
    acc_hi = p_self * vn_rows[:, hd:]
    l_elem = None
    for i in range(n_pages):
        p = jnp.exp(s_pages[i] - m)
        l_elem = p if l_elem is None else l_elem + p
        pb = p.astype(BF16)
        acc_lo = acc_lo + _dot(pb, flat(v_refs[i], 0))
        acc_hi = acc_hi + _dot(pb, flat(v_refs[i], 1))
    l = p_self + jnp.sum(l_elem, axis=-1, keepdims=True)
    n = jnp.concatenate([acc_lo, acc_hi], axis=1) / l
    lam = _lambda(lam_ref, lam_init)
    for kv in range(nkv):
        r0 = kv * 2 * grp
        o = _sub_norm(n[r0:r0 + grp] - lam * n[r0 + grp:r0 + 2 * grp], sub_ref[...], lam_init)
        for g in range(grp):
            c0 = (kv * grp + g) * vd
            o_ref[0, :, c0:c0 + vd] = o[g:g + 1]


def _paged_attention(page_table, qbd, k_new, v_new, cache_k, cache_v, layer, sbias, sbias_self, lam_vecs,
                     subln_w, lam_init):
    bs, n_pages = page_table.shape
    page = cache_k.shape[2]
    kw = 2 * ATTN_HD
    grp, vd = ATTN_GROUP, ATTN_VD

    rows = N_KV_HEADS * 2 * grp

    def page_spec(width, i):
        return pl.BlockSpec((None, None, page, N_KV_HEADS, width), lambda b, pt: (layer, pt[b, i], 0, 0, 0))

    grid_spec = pltpu.PrefetchScalarGridSpec(
        num_scalar_prefetch=1,
        grid=(bs,),
        in_specs=[
            pl.BlockSpec((1, rows, kw), lambda b, pt: (b, 0, 0)),
            pl.BlockSpec((1, N_KV_HEADS, kw), lambda b, pt: (b, 0, 0)),
            pl.BlockSpec((1, N_KV_HEADS, vd), lambda b, pt: (b, 0, 0)),
            pl.BlockSpec((n_pages, rows, page * N_KV_HEADS), lambda b, pt: (0, 0, 0)),
            pl.BlockSpec((rows, LANES), lambda b, pt: (0, 0)),
            pl.BlockSpec((4, ATTN_HD), lambda b, pt: (0, 0)),
            pl.BlockSpec((1, vd), lambda b, pt: (0, 0)),
        ] + [page_spec(kw, i) for i in range(n_pages)] + [page_spec(vd, i) for i in range(n_pages)],
        out_specs=pl.BlockSpec((1, 1, N_ATTN_HEADS * vd), lambda b, pt: (b, 0, 0)),
    )
    return pl.pallas_call(
        functools.partial(_paged_body, n_pages=n_pages, scale=ATTN_HD ** -0.5, lam_init=lam_init),
        grid_spec=grid_spec,
        out_shape=jax.ShapeDtypeStruct((bs, 1, N_ATTN_HEADS * vd), F32),
        compiler_params=_cparams(("parallel",)),
    )(page_table, qbd, k_new, v_new, sbias, sbias_self, lam_vecs, subln_w, *([cache_k] * n_pages),
      *([cache_v] * n_pages))


def _softplus(x):
    return jnp.maximum(x, 0.0) + jnp.log(1.0 + jnp.exp(-jnp.abs(x)))


def _l2norm(x):
    return x * lax.rsqrt(jnp.sum(x * x, -1, keepdims=True) + 1e-6)


def _split_bf16(x):
    hi = x.astype(BF16)
    return hi, (x - hi.astype(F32)).astype(BF16)


def _unit_lower_inverse(a, nilpotency):
    n = a.shape[0]
    eye = (lax.broadcasted_iota(I32, (n, n), 0) == lax.broadcasted_iota(I32, (n, n), 1)).astype(F32)
    y = -a
    r = eye + y
    span = 2
    while span < nilpotency:
        y_hi, y_lo = _split_bf16(y)
        y = _dot(y_hi, y_hi) + (_dot(y_hi, y_lo) + _dot(y_lo, y_hi))
        y_hi, y_lo = _split_bf16(y)
        r_hi, r_lo = _split_bf16(r)
        r = r + (_dot(r_hi, y_hi) + (_dot(r_hi, y_lo) + _dot(r_lo, y_hi)))
        span *= 2
    return r


def _gdn_body(q_ref, k_ref, v_ref, qp_ref, kp_ref, vp_ref, z_ref, ba_ref, cwq_ref, cwk_ref, cwv_ref,
              alog_ref, dtb_ref, nw_ref, o_ref, sout_ref, s_s, *, hg_size, chunk):
    hg = pl.program_id(1)
    c = pl.program_id(2)
    first = c == 0
    dk, dv = GDN_DK, GDN_DV

    @pl.when(first)
    def _():
        s_s[...] = jnp.zeros(s_s.shape, F32)

    def conv_silu(cur_ref, prev_ref, w_ref):
        cur = cur_ref[...]
        prev = jnp.where(first, 0.0, prev_ref[...])
        ext = jnp.concatenate([prev, cur], axis=0)
        y = cur * w_ref[GDN_CONV - 1:GDN_CONV, :]
        for d in range(1, GDN_CONV):
            shifted = pltpu.roll(ext, d, 0)[SUBLANES:SUBLANES + chunk]
            y = y + shifted * w_ref[GDN_CONV - 1 - d:GDN_CONV - d, :]
        return y * _sigmoid(y)

    qc = conv_silu(q_ref, qp_ref, cwq_ref)
    kc = conv_silu(k_ref, kp_ref, cwk_ref)
    vc = conv_silu(v_ref, vp_ref, cwv_ref)

    raw = ba_ref[...]
    off = hg * hg_size
    b_raw = pltpu.roll(raw, (LANES - off) % LANES, 1)
    a_raw = pltpu.roll(raw, (2 * LANES - GDN_HEADS - off) % LANES, 1)
    beta_all = _sigmoid(b_raw)
    g_all = -jnp.exp(alog_ref[0]) * _softplus(a_raw + dtb_ref[0])
    rows = lax.broadcasted_iota(I32, g_all.shape, 0)
    gc_all = g_all
    span = 1
    while span < chunk:
        gc_all = gc_all + jnp.where(rows >= span, pltpu.roll(gc_all, span, 0), 0.0)
        span *= 2
    gc_t = gc_all.T
    eg_all = jnp.exp(gc_all)

    nst = GDN_STACK
    rows_g = nst * chunk
    ri = lax.broadcasted_iota(I32, (rows_g, rows_g), 0)
    ci = lax.broadcasted_iota(I32, (rows_g, rows_g), 1)
    same_head = (ri // chunk) == (ci // chunk)
    causal = same_head & (ri >= ci)
    strict = same_head & (ri > ci)

    def stack(x_all, width, h0):
        return jnp.concatenate([x_all[:, (h0 + s) * width:(h0 + s + 1) * width] for s in range(nst)], axis=0)

    z_all = z_ref[...]
    for h0 in range(0, hg_size, nst):
        q = _l2norm(stack(qc, dk, h0)) * dk ** -0.5
        k = _l2norm(stack(kc, dk, h0))
        v = stack(vc, dv, h0)
        beta = stack(beta_all, 1, h0)
        gcol = stack(gc_all, 1, h0)
        egc = stack(eg_all, 1, h0)
        grow = jnp.concatenate([gc_t[h0 + s:h0 + s + 1, :] for s in range(nst)], axis=1)
        decay = jnp.where(causal, jnp.exp(jnp.where(causal, gcol - grow, 0.0)), 0.0)
        kb = k * beta
        kbf = k.astype(BF16)
        p = _dot_nt(jnp.concatenate([kb, q], axis=0).astype(BF16), kbf)
        a_mat = jnp.where(strict, p[:rows_g] * decay, 0.0)
        qk = p[rows_g:] * decay
        t_mat = _unit_lower_inverse(a_mat, chunk).astype(BF16)
        uw = _dot(t_mat, jnp.concatenate([v * beta, kb * egc], axis=1).astype(BF16))
        u, w = uw[:, :dv], uw[:, dv:]
        qd = q * egc
        ws, qs = [], []
        for s in range(nst):
            r0, r1 = s * chunk, (s + 1) * chunk
            both = _dot(jnp.concatenate([w[r0:r1], qd[r0:r1]], axis=0).astype(BF16), s_s[h0 + s].astype(BF16))
            ws.append(both[:chunk])
            qs.append(both[chunk:])
        v_new = u - jnp.concatenate(ws, axis=0)
        v_new_bf = v_new.astype(BF16)
        o = jnp.concatenate(qs, axis=0) + _dot(qk.astype(BF16), v_new_bf)
        for s in range(nst):
            r0, r1 = s * chunk, (s + 1) * chunk
            g_last = gcol[r1 - 1:r1, :]
            k_dec = (k[r0:r1] * jnp.exp(g_last - gcol[r0:r1])).astype(BF16)
            s_s[h0 + s] = s_s[h0 + s] * jnp.exp(g_last) + _dot_tn(k_dec, v_new_bf[r0:r1])
        z = stack(z_all, dv, h0)
        on = o * lax.rsqrt(jnp.mean(o * o, -1, keepdims=True) + 1e-6) * nw_ref[...] * (z * _sigmoid(z))
        for s in range(nst):
            o_ref[:, (h0 + s) * dv:(h0 + s + 1) * dv] = on[s * chunk:(s + 1) * chunk].astype(o_ref.dtype)

    @pl.when(c == pl.num_programs(2) - 1)
    def _():
        sout_ref[0] = s_s[...]


def _gdn_prompt(proj, ba, conv_w, alog_g, dtb_g, norm_w, bsz, seq, col0):
    hgs, chunk = GDN_HEAD_GROUP, GDN_CHUNK
    n_hg = GDN_HEADS // hgs
    w = hgs * GDN_DK
    n_chunks = seq // chunk
    hw = GDN_HEADS * GDN_DK
    assert col0 % w == 0 and seq % chunk == 0
    qb, kb, vb, zb = (col0 // w + i * (hw // w) for i in range(4))
    cpt = chunk // SUBLANES

    def cur(base):
        return pl.BlockSpec((chunk, w), lambda b, hg, c: (b * n_chunks + c, base + hg))

    def prev(base):
        return pl.BlockSpec((SUBLANES, w),
                            lambda b, hg, c: (jnp.maximum((b * n_chunks + c) * cpt - 1, 0), base + hg))

    def cw(base):
        return pl.BlockSpec((GDN_CONV, w), lambda b, hg, c: (0, base + hg))

    return pl.pallas_call(
        functools.partial(_gdn_body, hg_size=hgs, chunk=chunk),
        grid=(bsz, n_hg, n_chunks),
        in_specs=[cur(qb), cur(kb), cur(vb), prev(qb), prev(kb), prev(vb), cur(zb),
                  pl.BlockSpec((chunk, LANES), lambda b, hg, c: (b * n_chunks + c, 0)),
                  cw(0), cw(hw // w), cw(2 * hw // w),
                  pl.BlockSpec((1, 1, LANES), lambda b, hg, c: (hg, 0, 0)),
                  pl.BlockSpec((1, 1, LANES), lambda b, hg, c: (hg, 0, 0)),
                  pl.BlockSpec((1, GDN_DV), lambda b, hg, c: (0, 0))],
        out_specs=[pl.BlockSpec((chunk, w), lambda b, hg, c: (b * n_chunks + c, hg)),
                   pl.BlockSpec((1, hgs, GDN_DK, GDN_DV), lambda b, hg, c: (b, hg, 0, 0))],
        out_shape=[jax.ShapeDtypeStruct((bsz * seq, hw), BF16),
                   jax.ShapeDtypeStruct((bsz, GDN_HEADS, GDN_DK, GDN_DV), F32)],
        scratch_shapes=[pltpu.VMEM((hgs, GDN_DK, GDN_DV), F32)],
        compiler_params=_cparams(("parallel", "parallel", "arbitrary")),
    )(proj, proj, proj, proj, proj, proj, proj, ba, conv_w, conv_w, conv_w, alog_g, dtb_g, norm_w)


def _gdn_step_body(new_ref, buf_ref, cw_ref, z_ref, b_ref, a_ref, alog_ref, dtb_ref, nw_ref, s_ref,
                   o_ref, sout_ref, cout_ref):
    nh = GDN_HEADS
    new = new_ref[0]
    y = new * cw_ref[GDN_CONV - 1]
    for j in range(GDN_CONV - 1):
        y = y + buf_ref[0, j] * cw_ref[j]
        if j > 0:
            cout_ref[0, j - 1] = buf_ref[0, j]
    cout_ref[0, GDN_CONV - 2] = new
    y = y * _sigmoid(y)
    q = _l2norm(y[0:nh]) * GDN_DK ** -0.5
    k = _l2norm(y[nh:2 * nh])
    v = y[2 * nh:3 * nh]
    beta = _sigmoid(b_ref[0])
    eg = jnp.exp(-jnp.exp(alog_ref[...]) * _softplus(a_ref[0] + dtb_ref[...]))
    q_t = q.T
    k_t = k.T
    z = z_ref[0]
    for h in range(nh):
        s_old = s_ref[0, h]
        kcol = k_t[:, h:h + 1]
        egh = eg[h:h + 1, :]
        sk = jnp.sum(kcol * s_old, axis=0, keepdims=True)
        v_new = beta[h:h + 1, :] * (v[h:h + 1, :] - egh * sk)
        s_new = s_old * egh + kcol * v_new
        sout_ref[0, h] = s_new
        o = jnp.sum(q_t[:, h:h + 1] * s_new, axis=0, keepdims=True)
        zh = z[h:h + 1, :]
        on = o * lax.rsqrt(jnp.mean(o * o, -1, keepdims=True) + 1e-6) * nw_ref[...] * (zh * _sigmoid(zh))
        o_ref[0, h:h + 1, :] = on


def _gdn_step(qkv_new, conv_buf, conv_w, z, b_in, a_in, a_log, dt_bias, norm_w, state):
    bs = qkv_new.shape[0]
    nh, nr = GDN_HEADS, 3 * GDN_HEADS
    per_b3 = lambda shape: pl.BlockSpec((1,) + shape, lambda b: (b, 0, 0))
    per_b4 = lambda shape: pl.BlockSpec((1,) + shape, lambda b: (b, 0, 0, 0))
    return pl.pallas_call(
        _gdn_step_body,
        grid=(bs,),
        in_specs=[per_b3((nr, GDN_DK)), per_b4((GDN_CONV - 1, nr, GDN_DK)),
                  pl.BlockSpec((GDN_CONV, nr, GDN_DK), lambda b: (0, 0, 0)),
                  per_b3((nh, GDN_DV)), per_b3((nh, 1)), per_b3((nh, 1)),
                  pl.BlockSpec((nh, 1), lambda b: (0, 0)), pl.BlockSpec((nh, 1), lambda b: (0, 0)),
                  pl.BlockSpec((1, GDN_DV), lambda b: (0, 0)),
                  per_b4((nh, GDN_DK, GDN_DV))],
        out_specs=[per_b3((nh, GDN_DV)), per_b4((nh, GDN_DK, GDN_DV)), per_b4((GDN_CONV - 1, nr, GDN_DK))],
        out_shape=[jax.ShapeDtypeStruct((bs, nh, GDN_DV), F32),
                   jax.ShapeDtypeStruct((bs, nh, GDN_DK, GDN_DV), F32),
                   jax.ShapeDtypeStruct((bs, GDN_CONV - 1, nr, GDN_DK), F32)],
        compiler_params=_cparams(("parallel",)),
    )(qkv_new, conv_buf, conv_w, z, b_in, a_in, a_log, dt_bias, norm_w, state)


def _lane_partner(x, lane, s):
    return jnp.where((lane & s) != 0, pltpu.roll(x, s, 1), pltpu.roll(x, LANES - s, 1))


def _group_reduce(x, lane, op):
    s = 1
    while s < GROUP_SIZE:
        x = op(x, _lane_partner(x, lane, s))
        s *= 2
    return x


def _router_body(hp_ref, rw_ref, rb_ref, idx_ref, w_ref, rank_ref, cnt_ref, carry_s):
    i = pl.program_id(0)

    @pl.when(i == 0)
    def _():
        carry_s[...] = jnp.zeros(carry_s.shape, F32)

    half = rw_ref.shape[0] // 2
    lo, hi = _unpack_bf16_pairs(_load_slabs(hp_ref, ROW_TILE, half // LANES))
    rw = rw_ref[...].astype(BF16)
    scores = _sigmoid(_dot(lo, rw[:half]) + _dot(hi, rw[half:]))
    choice = scores + rb_ref[...]
    tm = scores.shape[0]
    lane = lax.broadcasted_iota(I32, (tm, LANES), 1)
    grp = lane // GROUP_SIZE
    big = jnp.int32(2 * LANES)

    m1 = _group_reduce(choice, lane, jnp.maximum)
    first = _group_reduce(jnp.where(choice == m1, lane, big), lane, jnp.minimum)
    m2 = _group_reduce(jnp.where(lane == first, NEG_INF, choice), lane, jnp.maximum)
    gs = m1 + m2
    beaten = jnp.zeros((tm, LANES), I32)
    for d in range(1, N_GROUPS):
        other = pltpu.roll(gs, d * GROUP_SIZE, 1)
        other_grp = pltpu.roll(grp, d * GROUP_SIZE, 1)
        beats = (other > gs) | ((other == gs) & (other_grp < grp))
        beaten = beaten + beats.astype(I32)
    masked = jnp.where(beaten < TOPK_GROUPS, choice, NEG_INF)

    idx_out = jnp.zeros((tm, LANES), I32)
    w_out = jnp.zeros((tm, LANES), F32)
    onehot = jnp.zeros((tm, LANES), F32)
    sels = []
    for k in range(TOP_K):
        m = jnp.max(masked, axis=-1, keepdims=True)
        idx = jnp.min(jnp.where(masked == m, lane, big), axis=-1, keepdims=True)
        sel = lane == idx
        sels.append(sel)
        wk = jnp.sum(jnp.where(sel, scores, 0.0), axis=-1, keepdims=True)
        idx_out = jnp.where(lane == k, idx, idx_out)
        w_out = jnp.where(lane == k, wk, w_out)
        onehot = jnp.where(sel, 1.0, onehot)
        masked = jnp.where(sel, NEG_INF, masked)
    w_out = w_out / jnp.sum(w_out, axis=-1, keepdims=True) * ROUTED_SCALE

    ri = lax.broadcasted_iota(I32, (tm, tm), 0)
    ci = lax.broadcasted_iota(I32, (tm, tm), 1)
    before = _dot((ri > ci).astype(BF16), onehot.astype(BF16)) + carry_s[...]
    rank_out = jnp.zeros((tm, LANES), F32)
    for k in range(TOP_K):
        rk = jnp.sum(jnp.where(sels[k], before, 0.0), axis=-1, keepdims=True)
        rank_out = jnp.where(lane == k, rk, rank_out)
    carry_s[...] = carry_s[...] + jnp.sum(onehot, axis=0, keepdims=True)

    idx_ref[...] = idx_out
    w_ref[...] = w_out
    rank_ref[...] = rank_out.astype(I32)
    cnt_ref[...] = jnp.broadcast_to(carry_s[...], cnt_ref.shape).astype(I32)


def _router(h_slabs, router_w, router_bias):
    d = router_w.shape[0]
    pitch = d // 2 // LANES
    t = h_slabs.shape[0] // pitch
    row = pl.BlockSpec((ROW_TILE, LANES), lambda i: (i, 0))
    return pl.pallas_call(
        _router_body,
        grid=(t // ROW_TILE,),
        in_specs=[pl.BlockSpec((ROW_TILE * pitch, LANES), lambda i: (i, 0)),
                  pl.BlockSpec((d, N_EXPERTS), lambda i: (0, 0)),
                  pl.BlockSpec((1, N_EXPERTS), lambda i: (0, 0))],
        out_specs=[row, row, row, pl.BlockSpec((SUBLANES, LANES), lambda i: (0, 0))],
        out_shape=[jax.ShapeDtypeStruct((t, LANES), I32), jax.ShapeDtypeStruct((t, LANES), F32),
                   jax.ShapeDtypeStruct((t, LANES), I32), jax.ShapeDtypeStruct((SUBLANES, LANES), I32)],
        scratch_shapes=[pltpu.VMEM((1, LANES), F32)],
        compiler_params=_cparams(("arbitrary",)),
    )(h_slabs, router_w, router_bias)


def _gather_body(used_ref, idx_ref, tab_ref, out_ref, sem, *, rows, n, src_pitch):
    def slab_copy(src, dst):
        return pltpu.make_async_copy(tab_ref.at[pl.ds(src * src_pitch, n)],
                                     out_ref.at[pl.ds(pl.multiple_of(dst * n, n), n)], sem)

    def issue(pair, carry):
        for lane in range(2):
            r = 2 * pair + lane
            slab_copy(idx_ref[0, 0, r], r).start(priority=lane)
        return carry

    def drain(r, carry):
        slab_copy(0, 0).wait()
        return carry

    in_use = pl.program_id(0) * rows < used_ref[0]

    @pl.when(in_use)
    def _():
        lax.fori_loop(0, rows // 2, issue, 0)
        lax.fori_loop(0, rows, drain, 0)

    @pl.when(jnp.logical_not(in_use))
    def _():
        out_ref[...] = jnp.zeros(out_ref.shape, out_ref.dtype)


def _slab_gather(table, idx, n, n_used, src_pitch=None):
    src_pitch = n if src_pitch is None else src_pitch
    count = idx.shape[0]
    rows = GATHER_ROWS
    assert count % rows == 0 and rows % 2 == 0
    return pl.pallas_call(
        functools.partial(_gather_body, rows=rows, n=n, src_pitch=src_pitch),
        grid_spec=pltpu.PrefetchScalarGridSpec(
            num_scalar_prefetch=1,
            grid=(count // rows,),
            in_specs=[pl.BlockSpec((1, 1, rows), lambda i, used: (i, 0, 0), memory_space=pltpu.SMEM),
                      pl.BlockSpec(memory_space=pl.ANY)],
            out_specs=pl.BlockSpec((rows * n, LANES), lambda i, used: (i, 0)),
            scratch_shapes=[pltpu.SemaphoreType.DMA(())],
        ),
        out_shape=jax.ShapeDtypeStruct((count * n, LANES), table.dtype),
        compiler_params=_cparams(("arbitrary",)),
    )(jnp.asarray(n_used, I32).reshape(1), idx.reshape(count // rows, 1, rows), table)


def _new_expert(be_ref, i):
    return (i == 0) | (be_ref[i] != be_ref[jnp.maximum(i - 1, 0)])


def _ffn_up_body(be_ref, bv_ref, x_ref, wg_ref, wu_ref, o_ref, wg_s, wu_s):
    i = pl.program_id(1)

    @pl.when(_new_expert(be_ref, i))
    def _():
        wg_s[...] = wg_ref[...].astype(BF16)
        wu_s[...] = wu_ref[...].astype(BF16)

    @pl.when(bv_ref[i] > 0)
    def _():
        half = wg_s.shape[0] // 2
        lo, hi = _unpack_bf16_pairs(_load_slabs(x_ref, o_ref.shape[0], half // LANES))
        g = _dot(lo, wg_s[:half]) + _dot(hi, wg_s[half:])
        u = _dot(lo, wu_s[:half]) + _dot(hi, wu_s[half:])
        o_ref[...] = (g * _sigmoid(g) * u).astype(o_ref.dtype)

    @pl.when(bv_ref[i] == 0)
    def _():
        o_ref[...] = jnp.zeros(o_ref.shape, o_ref.dtype)


def _ffn_down_body(be_ref, bv_ref, a_ref, wd_ref, o_ref, *, slab_pitch, k_chunk):
    i = pl.program_id(1)

    @pl.when(bv_ref[i] > 0)
    def _():
        a = a_ref[...]
        acc = None
        for c0 in range(0, a.shape[1], k_chunk):
            part = _dot(a[:, c0:c0 + k_chunk], wd_ref[c0:c0 + k_chunk, :].astype(BF16))
            acc = part if acc is None else acc + part
        if slab_pitch:
            _store_slabs(o_ref, acc, slab_pitch)
        else:
            o_ref[...] = acc

    @pl.when(bv_ref[i] == 0)
    def _():
        o_ref[...] = jnp.zeros(o_ref.shape, o_ref.dtype)


def _ffn_up(x_slabs, blocks, w_gate, w_up, layer, tm, tf):
    block_e, block_valid, block_row = blocks
    d = w_gate.shape[-2]
    ff = w_gate.shape[-1]
    pitch = d // 2 // LANES
    n_rows = x_slabs.shape[0] // pitch
    nb = block_e.shape[0]
    up = pl.pallas_call(
        _ffn_up_body,
        grid_spec=pltpu.PrefetchScalarGridSpec(
            num_scalar_prefetch=2,
            grid=(ff // tf, nb),
            in_specs=[pl.BlockSpec((tm * pitch, LANES), lambda j, i, be, bv: (bv[nb + i], 0)),
                      pl.BlockSpec((None, None, d, tf), lambda j, i, be, bv: (layer, be[i], 0, j)),
                      pl.BlockSpec((None, None, d, tf), lambda j, i, be, bv: (layer, be[i], 0, j))],
            out_specs=pl.BlockSpec((tm, tf), lambda j, i, be, bv: (i, j)),
            scratch_shapes=[pltpu.VMEM((d, tf), BF16), pltpu.VMEM((d, tf), BF16)],
        ),
        out_shape=jax.ShapeDtypeStruct((n_rows, ff), BF16),
        compiler_params=_cparams(("arbitrary", "arbitrary")),
    )
    return up(block_e, jnp.concatenate([block_valid, block_row]), x_slabs, w_gate, w_up)


def _ffn_down(act, blocks, w_down, layer, tm, tn, slab_pitch=0):
    block_e, block_valid, block_row = blocks
    n_rows, ff = act.shape
    d = w_down.shape[-1]
    nb = block_e.shape[0]
    assert not slab_pitch or tn == d
    bv = jnp.concatenate([block_valid, block_row])
    if slab_pitch:
        out_spec = pl.BlockSpec((tm * slab_pitch, LANES), lambda j, i, be, bv: (i, 0))
        out_shape = jax.ShapeDtypeStruct((n_rows * slab_pitch, LANES), F32)
    else:
        out_spec = pl.BlockSpec((tm, tn), lambda j, i, be, bv: (i, j))
        out_shape = jax.ShapeDtypeStruct((n_rows, d), F32)
    down = pl.pallas_call(
        functools.partial(_ffn_down_body, slab_pitch=slab_pitch, k_chunk=min(ff, 256)),
        grid_spec=pltpu.PrefetchScalarGridSpec(
            num_scalar_prefetch=2,
            grid=(d // tn, nb),
            in_specs=[pl.BlockSpec((tm, ff), lambda j, i, be, bv: (bv[nb + i], 0)),
                      pl.BlockSpec((None, None, ff, tn), lambda j, i, be, bv: (layer, be[i], 0, j))],
            out_specs=out_spec,
        ),
        out_shape=out_shape,
        compiler_params=_cparams(("arbitrary", "arbitrary")),
    )
    return down(block_e, bv, act, w_down)


def _moe(h_slabs, p, layer, t):
    d = p["router_w"].shape[0]
    top_idx, top_w, rank, counts = _router(h_slabs, p["router_w"], p["router_bias"])
    tm = MOE_ROW_BLOCK
    counts = counts[0]
    padded = (counts + tm - 1) // tm * tm
    pad_end = jnp.cumsum(padded)
    pad_start = pad_end - padded
    experts = jnp.arange(N_EXPERTS, dtype=I32)
    start_of = jnp.sum(jnp.where(top_idx[:, :TOP_K, None] == experts, pad_start, 0), axis=-1)
    pos = (start_of + rank[:, :TOP_K]).reshape(-1)
    n_blocks = (t * TOP_K) // tm + N_EXPERTS
    while (n_blocks * tm) % GATHER_ROWS:
        n_blocks += 1
    n_slots = n_blocks * tm
    slot_tok = jnp.zeros((n_slots,), I32).at[pos].set(jnp.repeat(jnp.arange(t, dtype=I32), TOP_K))
    n_used_rows = pad_end[-1]

    def blocks_of(rows_per_block):
        nb = n_blocks * (tm // rows_per_block)
        starts = jnp.arange(nb, dtype=I32) * rows_per_block
        blk = jnp.minimum(jnp.arange(nb, dtype=I32), n_used_rows // rows_per_block - 1)
        e = jnp.minimum(jnp.searchsorted(pad_end, blk * rows_per_block, side="right"), N_EXPERTS - 1).astype(I32)
        real_end = (pad_start + counts)[e]
        valid = ((starts < n_used_rows) & (starts < real_end)).astype(I32)
        return e, valid, blk

    x_sorted = _slab_gather(h_slabs, slot_tok, d // 2 // LANES, n_used_rows)
    y_pitch = d // LANES + Y_SLAB_PAD
    act = _ffn_up(x_sorted, blocks_of(tm), p["exp_gate"], p["exp_up"], layer, tm=tm,
                  tf=min(256, p["exp_gate"].shape[-1]))
    dtm = tm // 2
    y_sorted = _ffn_down(act, blocks_of(dtm), p["exp_down"], layer, tm=dtm, tn=d, slab_pitch=y_pitch)
    y_tok = _slab_gather(y_sorted, pos, d // LANES, t * TOP_K, src_pitch=y_pitch)

    stm = _pick_tm(t, 1024)
    nsb = t // stm
    one = jnp.ones((nsb,), I32)
    sblocks = (0 * one, one, jnp.arange(nsb, dtype=I32))
    lead = lambda w: w.reshape((w.shape[0], 1) + w.shape[1:])
    sact = _ffn_up(h_slabs, sblocks, lead(p["sh_gate"]), lead(p["sh_up"]), layer, tm=stm, tf=256)
    shared = _ffn_down(sact, sblocks, lead(p["sh_down"]), layer, tm=stm, tn=1024)
    return y_tok, top_w, shared


def kernel(x_prompt, x_sample, cache_k, cache_v, state_gdn, state_conv, page_table, c_prompt, c_sample, rel_bias, ada_w, ada_b, w_in, lam_q1, lam_k1, lam_q2, lam_k2, subln_w, conv_w, a_log, dt_bias, gdn_norm_w, w_branch, w_o, ln1_g, ln1_b, router_w, router_bias, exp_gate, exp_up, exp_down, sh_gate, sh_up, sh_down, ln2_g, ln2_b):
    bsz, seq, d = x_prompt.shape
    bs = x_sample.shape[0]
    depth = ada_w.shape[0]
    tp = bsz * seq
    t = tp + bs
    n_pages = page_table.shape[1]
    page = cache_k.shape[2]
    past = n_pages * page
    alpha = (2 * depth) ** 0.25
    assert bs == ROW_TILE and seq % ATTN_BLOCK == 0 and x_sample.shape[1] == 1

    x = jnp.concatenate([x_prompt.reshape(tp, d), x_sample.reshape(bs, d)], axis=0)
    c_all = jnp.concatenate([c_prompt, c_sample, jnp.zeros((-(bsz + bs) % SUBLANES, d), F32)], axis=0)

    q_w = N_ATTN_HEADS * 2 * ATTN_HD
    kv_w = N_KV_HEADS * 2 * ATTN_HD
    hw = GDN_HEADS * GDN_DK
    gdn0 = q_w + 2 * kv_w
    main_w = gdn0 + 4 * hw
    ba_w = 2 * GDN_HEADS

    blk = ATTN_BLOCK
    rel_line = jnp.arange(-(blk - 1), 2 * blk)
    line = jnp.where(rel_line >= 0, rel_bias[_t5_bucket(rel_line)].astype(F32).T, NEG_INF)
    bias_tiles = jnp.stack([_toeplitz(line[:, :2 * blk - 1]), _toeplitz(line[:, blk:])], axis=1)
    bias_far = rel_bias[N_BUCKETS - 1].astype(F32)
    rel_dec = past - jnp.arange(past)
    dec = rel_bias[_t5_bucket(rel_dec)].astype(F32).reshape(n_pages, page, N_KV_HEADS, ATTN_GROUP)
    dec = jnp.transpose(dec, (0, 2, 3, 1))[:, :, None, :, :, None]
    same_kv = jnp.eye(N_KV_HEADS, dtype=bool)[None, :, None, None, None, :]
    sbias = jnp.where(same_kv, jnp.broadcast_to(dec, (n_pages, N_KV_HEADS, 2, ATTN_GROUP, page, N_KV_HEADS)),
                      NEG_INF).reshape(n_pages, N_KV_HEADS * 2 * ATTN_GROUP, page * N_KV_HEADS)
    self_b = rel_bias[0].astype(F32).reshape(N_KV_HEADS, 1, ATTN_GROUP)
    sbias_self = jnp.broadcast_to(jnp.broadcast_to(self_b, (N_KV_HEADS, 2, ATTN_GROUP)).reshape(
        N_KV_HEADS * 2 * ATTN_GROUP, 1), (N_KV_HEADS * 2 * ATTN_GROUP, LANES))

    y = x
    outs = {n: [] for n in ("kp", "vp", "sp", "cp", "ks", "vs", "ss", "cs")}
    for l in range(depth):
        lam_init = 0.8 - 0.6 * math.exp(-0.3 * l)
        p = {"router_w": router_w[l], "router_bias": router_bias[l].reshape(1, -1), "exp_gate": exp_gate,
             "exp_up": exp_up, "exp_down": exp_down, "sh_gate": sh_gate, "sh_up": sh_up, "sh_down": sh_down}

        mod = _mm(c_all, ada_w[l], tm=c_all.shape[0], tn=512, n_out=6 * d, bias=ada_b[l].reshape(1, -1),
                  a_silu=True)
        mod_p = mod[:bsz].reshape(bsz, 1, 6 * d)
        mod_s = mod[bsz:bsz + bs]

        h = _modulate(y, mod_p, mod_s, 1, 0, seq)
        tm_big = _pick_tm(t, 1664)
        proj = _mm(h, w_in[l], tm=tm_big, tn=256, n_out=main_w)
        w_tail = w_in[l][:, main_w:]
        w_ba = jnp.pad(w_tail[:, :ba_w], ((0, 0), (0, LANES - ba_w)))
        ba = _mm(h, w_ba, tm=tm_big, tn=LANES, n_out=LANES)
        gates = _mm(h, w_tail[:, ba_w:], tm=tm_big, tn=256, n_out=2 * d)

        lam_vecs = jnp.stack([lam_q1[l], lam_k1[l], lam_q2[l], lam_k2[l]]).astype(F32)
        sub_w = subln_w[l].reshape(1, -1).astype(F32)

        o_a_p = _flash_attention(proj, bias_tiles, bias_far, lam_vecs, sub_w, bsz, seq, lam_init)
        srow = proj[tp:]
        q_s = srow[:, :q_w].reshape(bs, N_KV_HEADS, ATTN_GROUP, 2, ATTN_HD)
        zeros = jnp.zeros_like(q_s[:, :, :, 0])
        qbd = jnp.stack([jnp.concatenate([q_s[:, :, :, 0], zeros], -1),
                         jnp.concatenate([zeros, q_s[:, :, :, 1]], -1)], axis=2)
        qbd = qbd.reshape(bs, N_KV_HEADS * 2 * ATTN_GROUP, 2 * ATTN_HD)
        k_s = srow[:, q_w:q_w + kv_w]
        v_s = srow[:, q_w + kv_w:gdn0]
        o_a_s = _paged_attention(page_table, qbd, k_s.reshape(bs, N_KV_HEADS, 2 * ATTN_HD),
                                 v_s.reshape(bs, N_KV_HEADS, ATTN_VD), cache_k, cache_v, l, sbias,
                                 sbias_self, lam_vecs, sub_w, lam_init)
        o_a = jnp.concatenate([o_a_p, o_a_s.reshape(bs, -1).astype(BF16)], axis=0)

        n_hg = GDN_HEADS // GDN_HEAD_GROUP
        group_lanes = lambda vec: jnp.pad(vec.astype(F32).reshape(n_hg, 1, GDN_HEAD_GROUP),
                                          ((0, 0), (0, 0), (0, LANES - GDN_HEAD_GROUP)))
        nw = gdn_norm_w[l].reshape(1, -1).astype(F32)
        o_g_p, s_p = _gdn_prompt(proj, ba, conv_w[l], group_lanes(a_log[l]), group_lanes(dt_bias[l]), nw,
                                 bsz, seq, gdn0)
        nr = 3 * GDN_HEADS
        o_g_s, s_s, c_s = _gdn_step(
            srow[:, gdn0:gdn0 + 3 * hw].reshape(bs, nr, GDN_DK),
            state_conv[l].reshape(bs, GDN_CONV - 1, nr, GDN_DK),
            conv_w[l].reshape(GDN_CONV, nr, GDN_DK),
            srow[:, gdn0 + 3 * hw:main_w].reshape(bs, GDN_HEADS, GDN_DV),
            ba[tp:, :GDN_HEADS].reshape(bs, GDN_HEADS, 1), ba[tp:, GDN_HEADS:ba_w].reshape(bs, GDN_HEADS, 1),
            a_log[l].reshape(-1, 1).astype(F32), dt_bias[l].reshape(-1, 1).astype(F32), nw, state_gdn[l])
        o_g = jnp.concatenate([o_g_p, o_g_s.reshape(bs, -1).astype(BF16)], axis=0)

        tm_mid = _pick_tm(t, 1024)
        m_a = _mm(o_a, w_branch[l], tm=tm_mid, tn=512, n_out=d, gate=gates)
        merged = _mm(o_g, w_branch[l], tm=tm_mid, tn=512, n_out=d, w_row_blk=1, gate=gates,
                     gate_col_blk=d // 512, prev=m_a, out_dtype=BF16)
        attn_out = _mm(merged, w_o[l], tm=tm_mid, tn=512, n_out=d)
        x1, h2p = _ln1(y, attn_out, mod_p, mod_s, ln1_g[l].reshape(1, -1), ln1_b[l].reshape(1, -1), seq, alpha)

        y_tok, top_w, shared = _moe(h2p, p, l, t)
        y = _final(x1, y_tok, top_w, shared, mod_p, mod_s, ln2_g[l].reshape(1, -1), ln2_b[l].reshape(1, -1),
                   seq, alpha)

        kcols = proj[:, q_w:q_w + kv_w]
        vcols = proj[:, q_w + kv_w:gdn0]
        outs["kp"].append(kcols[:tp].reshape(bsz, seq, N_KV_HEADS, 2 * ATTN_HD))
        outs["vp"].append(vcols[:tp].reshape(bsz, seq, N_KV_HEADS, ATTN_VD))
        outs["sp"].append(s_p)
        tail = jnp.stack([proj[(b + 1) * seq - (GDN_CONV - 1):(b + 1) * seq, gdn0:gdn0 + 3 * hw]
                          for b in range(bsz)])
        outs["cp"].append(tail)
        outs["ks"].append(kcols[tp:].reshape(bs, 1, N_KV_HEADS, 2 * ATTN_HD))
        outs["vs"].append(vcols[tp:].reshape(bs, 1, N_KV_HEADS, ATTN_VD))
        outs["ss"].append(s_s)
        outs["cs"].append(c_s.reshape(bs, GDN_CONV - 1, 3 * hw))

    st = lambda n: jnp.stack(outs[n])
    return (y[:tp].reshape(bsz, seq, d), y[tp:].reshape(bs, 1, d), st("kp"), st("vp"), st("sp"), st("cp"),
            st("ks"), st("vs"), st("ss"), st("cs"))
```

```python
import functools
import math

import jax
import jax.numpy as jnp
from jax import lax
from jax.experimental import pallas as pl
from jax.experimental.pallas import tpu as pltpu

F32 = jnp.float32
BF16 = jnp.bfloat16
U32 = jnp.uint32
I32 = jnp.int32

N_ATTN_HEADS = 16
N_KV_HEADS = 4
ATTN_GROUP = N_ATTN_HEADS // N_KV_HEADS
ATTN_HD = 128
ATTN_VD = 256
N_BUCKETS = 32
MAX_DISTANCE = 128
GDN_HEADS = 32
GDN_DK = 128
GDN_DV = 128
GDN_CONV = 4
GDN_CHUNK = 64
N_EXPERTS = 128
TOP_K = 8
N_GROUPS = 8
GROUP_SIZE = N_EXPERTS // N_GROUPS
TOPK_GROUPS = 4
ROUTED_SCALE = 2.5

LANES = 128
SUBLANES = 8
VMEM_LIMIT = 56 * 1024 * 1024

ROW_TILE = 128
FINAL_TILE = 64
ATTN_BLOCK = 256
GDN_HEAD_GROUP = 16
GDN_STACK = 4
MOE_ROW_BLOCK = 640
GATHER_ROWS = 256
Y_SLAB_PAD = 4

NEG_INF = float("-inf")


def _cparams(sem, vmem=VMEM_LIMIT):
    return pltpu.CompilerParams(dimension_semantics=sem, vmem_limit_bytes=vmem)


def _sigmoid(x):
    return jax.nn.sigmoid(x)


def _dot(a, b):
    return jnp.dot(a, b, preferred_element_type=F32)


def _dot_nt(a, b):
    return lax.dot_general(a, b, (((1,), (1,)), ((), ())), preferred_element_type=F32)


def _dot_tn(a, b):
    return lax.dot_general(a, b, (((0,), (0,)), ((), ())), preferred_element_type=F32)


def _mm_body(*refs, a_silu, has_bias, has_gate, has_prev):
    it = iter(refs)
    a_ref = next(it)
    w_ref = next(it)
    bias_ref = next(it) if has_bias else None
    gate_ref = next(it) if has_gate else None
    prev_ref = next(it) if has_prev else None
    o_ref = next(it)
    a = a_ref[...]
    if a_silu:
        a = a.astype(F32)
        a = a * _sigmoid(a)
    acc = _dot(a.astype(BF16), w_ref[...].astype(BF16))
    if has_bias:
        acc = acc + bias_ref[...]
    if has_gate:
        acc = _sigmoid(gate_ref[...]) * acc
    if has_prev:
        acc = acc + prev_ref[...]
    o_ref[...] = acc.astype(o_ref.dtype)


def _mm(a, w, *, tm, tn, n_out, w_row_blk=0, w_col_blk=0, bias=None, gate=None, gate_col_blk=0,
        prev=None, a_silu=False, out_dtype=F32):
    m, k = a.shape
    assert m % tm == 0 and n_out % tn == 0
    in_specs = [pl.BlockSpec((tm, k), lambda i, j: (i, 0)),
                pl.BlockSpec((k, tn), lambda i, j: (w_row_blk, j + w_col_blk))]
    args = [a, w]
    if bias is not None:
        in_specs.append(pl.BlockSpec((1, tn), lambda i, j: (0, j)))
        args.append(bias)
    if gate is not None:
        in_specs.append(pl.BlockSpec((tm, tn), lambda i, j: (i, j + gate_col_blk)))
        args.append(gate)
    if prev is not None:
        in_specs.append(pl.BlockSpec((tm, tn), lambda i, j: (i, j)))
        args.append(prev)
    body = functools.partial(_mm_body, a_silu=a_silu, has_bias=bias is not None,
                             has_gate=gate is not None, has_prev=prev is not None)
    return pl.pallas_call(
        body,
        grid=(m // tm, n_out // tn),
        in_specs=in_specs,
        out_specs=pl.BlockSpec((tm, tn), lambda i, j: (i, j)),
        out_shape=jax.ShapeDtypeStruct((m, n_out), out_dtype),
        compiler_params=_cparams(("parallel", "arbitrary")),
    )(*args)


def _pick_tm(m, cap):
    best = None
    for t in range(LANES, cap + 1, LANES):
        if m % t == 0:
            best = t
    assert best is not None
    return best


def _row_mod(i, n_prompt_tiles, p_ref, s_ref):
    return jnp.where(i >= n_prompt_tiles, s_ref[...], p_ref[0])


def _mod_specs(d, chunk, tiles_per_seq, n_prompt_tiles, tile=ROW_TILE):
    p_spec = pl.BlockSpec((1, 1, d), lambda i: (jnp.minimum(i, n_prompt_tiles - 1) // tiles_per_seq, 0, chunk))
    s_spec = pl.BlockSpec((tile, d), lambda i: (jnp.maximum(i - n_prompt_tiles, 0), chunk))
    return p_spec, s_spec


def _modulate_body(x_ref, scp_ref, scs_ref, shp_ref, shs_ref, o_ref, *, n_prompt_tiles):
    i = pl.program_id(0)
    sc = _row_mod(i, n_prompt_tiles, scp_ref, scs_ref)
    sh = _row_mod(i, n_prompt_tiles, shp_ref, shs_ref)
    o_ref[...] = (x_ref[...] * (1.0 + sc) + sh).astype(o_ref.dtype)


def _modulate(x, mod_p, mod_s, sc_chunk, sh_chunk, seq):
    t, d = x.shape
    n_prompt_tiles = mod_p.shape[0] * seq // ROW_TILE
    tps = seq // ROW_TILE
    scp, scs = _mod_specs(d, sc_chunk, tps, n_prompt_tiles)
    shp, shs = _mod_specs(d, sh_chunk, tps, n_prompt_tiles)
    return pl.pallas_call(
        functools.partial(_modulate_body, n_prompt_tiles=n_prompt_tiles),
        grid=(t // ROW_TILE,),
        in_specs=[pl.BlockSpec((ROW_TILE, d), lambda i: (i, 0)), scp, scs, shp, shs],
        out_specs=pl.BlockSpec((ROW_TILE, d), lambda i: (i, 0)),
        out_shape=jax.ShapeDtypeStruct((t, d), BF16),
        compiler_params=_cparams(("parallel",)),
    )(x, mod_p, mod_s, mod_p, mod_s)


def _layer_norm_rows(v, g, b):
    mu = jnp.mean(v, -1, keepdims=True)
    var = jnp.mean(jnp.square(v - mu), -1, keepdims=True)
    return (v - mu) * lax.rsqrt(var + 1e-5) * g + b


def _pack_bf16_pairs(h):
    half = h.shape[1] // 2
    bits = lax.bitcast_convert_type(h.astype(BF16).astype(F32), U32)
    return (bits[:, :half] >> 16) | (bits[:, half:] & jnp.uint32(0xFFFF0000))


def _unpack_bf16_pairs(p):
    lo = lax.bitcast_convert_type(p << 16, F32).astype(BF16)
    hi = lax.bitcast_convert_type(p & jnp.uint32(0xFFFF0000), F32).astype(BF16)
    return lo, hi


def _store_slabs(o_ref, v, pitch=None):
    rows, n = v.shape[0], v.shape[1] // LANES
    pitch = n if pitch is None else pitch
    for s in range(n):
        o_ref[pl.ds(s, rows, stride=pitch), :] = v[:, s * LANES:(s + 1) * LANES]
    for s in range(n, pitch):
        o_ref[pl.ds(s, rows, stride=pitch), :] = jnp.zeros((rows, LANES), v.dtype)


def _load_slabs(ref, rows, n, pitch=None, base=0):
    pitch = n if pitch is None else pitch
    return jnp.concatenate([ref[pl.ds(base + s, rows, stride=pitch), :] for s in range(n)], axis=1)


def _ln1_body(x_ref, y_ref, gp_ref, gs_ref, scp_ref, scs_ref, shp_ref, shs_ref, lg_ref, lb_ref,
              x1_ref, hp_ref, *, n_prompt_tiles, alpha):
    i = pl.program_id(0)
    g1 = _row_mod(i, n_prompt_tiles, gp_ref, gs_ref)
    sc = _row_mod(i, n_prompt_tiles, scp_ref, scs_ref)
    sh = _row_mod(i, n_prompt_tiles, shp_ref, shs_ref)
    x1 = _layer_norm_rows(alpha * x_ref[...] + g1 * y_ref[...], lg_ref[...], lb_ref[...])
    x1_ref[...] = x1
    _store_slabs(hp_ref, _pack_bf16_pairs(x1 * (1.0 + sc) + sh))


def _ln1(x, y, mod_p, mod_s, ln_g, ln_b, seq, alpha):
    t, d = x.shape
    n_prompt_tiles = mod_p.shape[0] * seq // ROW_TILE
    tps = seq // ROW_TILE
    gp, gs = _mod_specs(d, 2, tps, n_prompt_tiles)
    shp, shs = _mod_specs(d, 3, tps, n_prompt_tiles)
    scp, scs = _mod_specs(d, 4, tps, n_prompt_tiles)
    row = pl.BlockSpec((ROW_TILE, d), lambda i: (i, 0))
    vec = pl.BlockSpec((1, d), lambda i: (0, 0))
    return pl.pallas_call(
        functools.partial(_ln1_body, n_prompt_tiles=n_prompt_tiles, alpha=alpha),
        grid=(t // ROW_TILE,),
        in_specs=[row, row, gp, gs, scp, scs, shp, shs, vec, vec],
        out_specs=[row, pl.BlockSpec((ROW_TILE * (d // 2 // LANES), LANES), lambda i: (i, 0))],
        out_shape=[jax.ShapeDtypeStruct((t, d), F32), jax.ShapeDtypeStruct((t * (d // 2 // LANES), LANES), U32)],
        compiler_params=_cparams(("parallel",)),
    )(x, y, mod_p, mod_s, mod_p, mod_s, mod_p, mod_s, ln_g, ln_b)


def _final_body(x_ref, yt_ref, tw_ref, sh_ref, gp_ref, gs_ref, lg_ref, lb_ref, o_ref, *,
                n_prompt_tiles, alpha, d):
    i = pl.program_id(0)
    g2 = _row_mod(i, n_prompt_tiles, gp_ref, gs_ref)
    f = sh_ref[...]
    tw = tw_ref[...]
    rows, n = f.shape[0], d // LANES
    for k in range(TOP_K):
        f = f + _load_slabs(yt_ref, rows, n, pitch=TOP_K * n, base=k * n) * tw[:, k:k + 1]
    o_ref[...] = _layer_norm_rows(alpha * x_ref[...] + g2 * f, lg_ref[...], lb_ref[...])


def _final(x1, y_tok, top_w, shared, mod_p, mod_s, ln_g, ln_b, seq, alpha):
    t, d = x1.shape
    tile = FINAL_TILE
    n_prompt_tiles = mod_p.shape[0] * seq // tile
    gp, gs = _mod_specs(d, 5, seq // tile, n_prompt_tiles, tile)
    row = pl.BlockSpec((tile, d), lambda i: (i, 0))
    vec = pl.BlockSpec((1, d), lambda i: (0, 0))
    return pl.pallas_call(
        functools.partial(_final_body, n_prompt_tiles=n_prompt_tiles, alpha=alpha, d=d),
        grid=(t // tile,),
        in_specs=[row, pl.BlockSpec((tile * TOP_K * (d // LANES), LANES), lambda i: (i, 0)),
                  pl.BlockSpec((tile, LANES), lambda i: (i, 0)), row, gp, gs, vec, vec],
        out_specs=row,
        out_shape=jax.ShapeDtypeStruct((t, d), F32),
        compiler_params=_cparams(("parallel",)),
    )(x1, y_tok, top_w, shared, mod_p, mod_s, ln_g, ln_b)


def _t5_bucket(rel):
    n = jnp.maximum(rel, 0)
    max_exact = N_BUCKETS // 2
    nf = jnp.maximum(n, 1).astype(F32)
    large = max_exact + (jnp.log(nf / max_exact) / math.log(MAX_DISTANCE / max_exact)
                         * (N_BUCKETS - max_exact)).astype(I32)
    return jnp.where(n < max_exact, n, jnp.minimum(large, N_BUCKETS - 1))


def _toeplitz(a):
    n = (a.shape[-1] + 1) // 2
    zero = jnp.zeros(a.shape[:-1] + (1,), a.dtype)
    u = jnp.concatenate([a[..., n - 1::-1], zero, a[..., :n - 1:-1]], axis=-1)
    skew = jnp.tile(u, n)[..., :n * (2 * n - 1)].reshape(a.shape[:-1] + (n, 2 * n - 1))
    return skew[..., :n]


def _lambda(lam_ref, lam_init):
    lam = lam_ref[...]
    s1 = jnp.sum(lam[0:1] * lam[1:2], axis=-1, keepdims=True)
    s2 = jnp.sum(lam[2:3] * lam[3:4], axis=-1, keepdims=True)
    return jnp.exp(s1) - jnp.exp(s2) + lam_init


def _sub_norm(o, w, lam_init):
    return o * lax.rsqrt(jnp.mean(o * o, -1, keepdims=True) + 1e-5) * w * (1.0 - lam_init)


def _online_update(idx, s, v, m_s, l_s, acc_s):
    m_old = m_s[idx]
    m_new = jnp.maximum(m_old, jnp.max(s, axis=-1, keepdims=True))
    p = jnp.exp(s - m_new)
    corr = jnp.exp(m_old - m_new)
    l_s[idx] = corr * l_s[idx] + jnp.sum(p, axis=-1, keepdims=True)
    acc_s[idx] = corr * acc_s[idx] + _dot(p.astype(BF16), v)
    m_s[idx] = m_new


def _flash_body(far_ref, q_ref, k_ref, v_ref, bt_ref, lam_ref, sub_ref, o_ref, m_s, l_s, acc_s, *,
                scale, lam_init):
    kv = pl.program_id(1)
    qi = pl.program_id(2)
    ki = pl.program_id(3)
    hd, vd, grp = ATTN_HD, ATTN_VD, ATTN_GROUP

    @pl.when(ki == 0)
    def _():
        m_s[...] = jnp.full(m_s.shape, NEG_INF, F32)
        l_s[...] = jnp.zeros(l_s.shape, F32)
        acc_s[...] = jnp.zeros(acc_s.shape, F32)

    def process(get_bias):
        k = k_ref[...].astype(BF16)
        v = v_ref[...].astype(BF16)
        for g in range(grp):
            bias = get_bias(g)
            for j in range(2):
                c0 = g * 2 * hd + j * hd
                q = q_ref[:, c0:c0 + hd].astype(BF16)
                s = _dot_nt(q, k[:, j * hd:(j + 1) * hd]) * scale + bias
                _online_update(g * 2 + j, s, v, m_s, l_s, acc_s)

    @pl.when(ki < qi - 1)
    def _():
        process(lambda g: far_ref[kv * grp + g])

    @pl.when(ki == qi - 1)
    def _():
        process(lambda g: bt_ref[g, 1])

    @pl.when(ki == qi)
    def _():
        process(lambda g: bt_ref[g, 0])
        lam = _lambda(lam_ref, lam_init)
        for g in range(grp):
            o1 = acc_s[g * 2] / l_s[g * 2]
            o2 = acc_s[g * 2 + 1] / l_s[g * 2 + 1]
            o = _sub_norm(o1 - lam * o2, sub_ref[...], lam_init)
            o_ref[:, g * vd:(g + 1) * vd] = o.astype(o_ref.dtype)


def _flash_attention(proj, bias_tiles, bias_far, lam_vecs, subln_w, bsz, seq, lam_init):
    blk = ATTN_BLOCK
    nblk = seq // blk
    grp, hd, vd = ATTN_GROUP, ATTN_HD, ATTN_VD
    qw = grp * 2 * hd
    k_col0 = N_KV_HEADS * qw // (2 * hd)
    v_col0 = k_col0 + N_KV_HEADS
    grid_spec = pltpu.PrefetchScalarGridSpec(
        num_scalar_prefetch=0,
        grid=(bsz, N_KV_HEADS, nblk, nblk),
        in_specs=[
            pl.BlockSpec(memory_space=pltpu.SMEM),
            pl.BlockSpec((blk, qw), lambda b, kv, qi, ki: (b * nblk + qi, kv)),
            pl.BlockSpec((blk, 2 * hd), lambda b, kv, qi, ki: (b * nblk + jnp.minimum(ki, qi), k_col0 + kv)),
            pl.BlockSpec((blk, vd), lambda b, kv, qi, ki: (b * nblk + jnp.minimum(ki, qi), v_col0 + kv)),
            pl.BlockSpec((grp, 2, blk, blk), lambda b, kv, qi, ki: (kv, 0, 0, 0)),
            pl.BlockSpec((4, hd), lambda b, kv, qi, ki: (0, 0)),
            pl.BlockSpec((1, vd), lambda b, kv, qi, ki: (0, 0)),
        ],
        out_specs=pl.BlockSpec((blk, grp * vd), lambda b, kv, qi, ki: (b * nblk + qi, kv)),
        scratch_shapes=[pltpu.VMEM((2 * grp, blk, 1), F32), pltpu.VMEM((2 * grp, blk, 1), F32),
                        pltpu.VMEM((2 * grp, blk, vd), F32)],
    )
    return pl.pallas_call(
        functools.partial(_flash_body, scale=hd ** -0.5, lam_init=lam_init),
        grid_spec=grid_spec,
        out_shape=jax.ShapeDtypeStruct((bsz * seq, N_ATTN_HEADS * vd), BF16),
        compiler_params=_cparams(("parallel", "parallel", "parallel", "arbitrary")),
    )(bias_far, proj, proj, proj, bias_tiles, lam_vecs, subln_w)


def _paged_body(pt_ref, qbd_ref, kn_ref, vn_ref, sb_ref, sbs_ref, lam_ref, sub_ref, *rest, n_pages, scale,
                lam_init):
    k_refs = rest[:n_pages]
    v_refs = rest[n_pages:2 * n_pages]
    o_ref = rest[2 * n_pages]
    grp, vd, hd, nkv = ATTN_GROUP, ATTN_VD, ATTN_HD, N_KV_HEADS
    rows = nkv * 2 * grp
    q = qbd_ref[0].astype(BF16)
    q_lo, q_hi = q[:, :hd], q[:, hd:]
    qf = q.astype(F32)
    kn = kn_ref[0].astype(BF16).astype(F32)
    vn = vn_ref[0].astype(BF16).astype(F32)
    kn_rows = jnp.concatenate([jnp.broadcast_to(kn[kv:kv + 1], (2 * grp, 2 * hd)) for kv in range(nkv)], axis=0)
    vn_rows = jnp.concatenate([jnp.broadcast_to(vn[kv:kv + 1], (2 * grp, vd)) for kv in range(nkv)], axis=0)
    s_self = jnp.sum(qf * kn_rows, axis=-1, keepdims=True) * scale + sbs_ref[:, 0:1]

    def flat(ref, c):
        blk = ref[:, :, c * hd:(c + 1) * hd]
        return blk.reshape(blk.shape[0] * blk.shape[1], hd).astype(BF16)

    s_pages = []
    for i in range(n_pages):
        s = _dot_nt(q_lo, flat(k_refs[i], 0)) + _dot_nt(q_hi, flat(k_refs[i], 1))
        s_pages.append(s * scale + sb_ref[i])
    m_elem = s_pages[0]
    for s in s_pages[1:]:
        m_elem = jnp.maximum(m_elem, s)
    m = jnp.maximum(jnp.max(m_elem, axis=-1, keepdims=True), s_self)
    p_self = jnp.exp(s_self - m)
    acc_lo = p_self * vn_rows[:, :hd]
    acc_hi = p_self * vn_rows[:, hd:]
    l_elem = None
    for i in range(n_pages):
        p = jnp.exp(s_pages[i] - m)
        l_elem = p if l_elem is None else l_elem + p
        pb = p.astype(BF16)
        acc_lo = acc_lo + _dot(pb, flat(v_refs[i], 0))
        acc_hi = acc_hi + _dot(pb, flat(v_refs[i], 1))
    l = p_self + jnp.sum(l_elem, axis=-1, keepdims=True)
    n = jnp.concatenate([acc_lo, acc_hi], axis=1) / l
    lam = _lambda(lam_ref, lam_init)
    for kv in range(nkv):
        r0 = kv * 2 * grp
        o = _sub_norm(n[r0:r0 + grp] - lam * n[r0 + grp:r0 + 2 * grp], sub_ref[...], lam_init)
        for g in range(grp):
            c0 = (kv * grp + g) * vd
            o_ref[0, :, c0:c0 + vd] = o[g:g + 1]


def _paged_attention(page_table, qbd, k_new, v_new, cache_k, cache_v, layer, sbias, sbias_self, lam_vecs,
                     subln_w, lam_init):
    bs, n_pages = page_table.shape
    page = cache_k.shape[2]
    kw = 2 * ATTN_HD
    grp, vd = ATTN_GROUP, ATTN_VD

    rows = N_KV_HEADS * 2 * grp

    def page_spec(width, i):
        return pl.BlockSpec((None, None, page, N_KV_HEADS, width), lambda b, pt: (layer, pt[b, i], 0, 0, 0))

    grid_spec = pltpu.PrefetchScalarGridSpec(
        num_scalar_prefetch=1,
        grid=(bs,),
        in_specs=[
            pl.BlockSpec((1, rows, kw), lambda b, pt: (b, 0, 0)),
            pl.BlockSpec((1, N_KV_HEADS, kw), lambda b, pt: (b, 0, 0)),
            pl.BlockSpec((1, N_KV_HEADS, vd), lambda b, pt: (b, 0, 0)),
            pl.BlockSpec((n_pages, rows, page * N_KV_HEADS), lambda b, pt: (0, 0, 0)),
            pl.BlockSpec((rows, LANES), lambda b, pt: (0, 0)),
            pl.BlockSpec((4, ATTN_HD), lambda b, pt: (0, 0)),
            pl.BlockSpec((1, vd), lambda b, pt: (0, 0)),
        ] + [page_spec(kw, i) for i in range(n_pages)] + [page_spec(vd, i) for i in range(n_pages)],
        out_specs=pl.BlockSpec((1, 1, N_ATTN_HEADS * vd), lambda b, pt: (b, 0, 0)),
    )
    return pl.pallas_call(
        functools.partial(_paged_body, n_pages=n_pages, scale=ATTN_HD ** -0.5, lam_init=lam_init),
        grid_spec=grid_spec,
        out_shape=jax.ShapeDtypeStruct((bs, 1, N_ATTN_HEADS * vd), F32),
        compiler_params=_cparams(("parallel",)),
    )(page_table, qbd, k_new, v_new, sbias, sbias_self, lam_vecs, subln_w, *([cache_k] * n_pages),
      *([cache_v] * n_pages))


def _softplus(x):
    return jnp.maximum(x, 0.0) + jnp.log(1.0 + jnp.exp(-jnp.abs(x)))


def _l2norm(x):
    return x * lax.rsqrt(jnp.sum(x * x, -1, keepdims=True) + 1e-6)


def _split_bf16(x):
    hi = x.astype(BF16)
    return hi, (x - hi.astype(F32)).astype(BF16)


def _unit_lower_inverse(a, nilpotency):
    n = a.shape[0]
    eye = (lax.broadcasted_iota(I32, (n, n), 0) == lax.broadcasted_iota(I32, (n, n), 1)).astype(F32)
    y = -a
    r = eye + y
    span = 2
    while span < nilpotency:
        y_hi, y_lo = _split_bf16(y)
        y = _dot(y_hi, y_hi) + (_dot(y_hi, y_lo) + _dot(y_lo, y_hi))
        y_hi, y_lo = _split_bf16(y)
        r_hi, r_lo = _split_bf16(r)
        r = r + (_dot(r_hi, y_hi) + (_dot(r_hi, y_lo) + _dot(r_lo, y_hi)))
        span *= 2
    return r


def _gdn_body(q_ref, k_ref, v_ref, qp_ref, kp_ref, vp_ref, z_ref, ba_ref, cwq_ref, cwk_ref, cwv_ref,
              alog_ref, dtb_ref, nw_ref, o_ref, sout_ref, s_s, *, hg_size, chunk):
    hg = pl.program_id(1)
    c = pl.program_id(2)
    first = c == 0
    dk, dv = GDN_DK, GDN_DV

    @pl.when(first)
    def _():
        s_s[...] = jnp.zeros(s_s.shape, F32)

    def conv_silu(cur_ref, prev_ref, w_ref):
        cur = cur_ref[...]
        prev = jnp.where(first, 0.0, prev_ref[...])
        ext = jnp.concatenate([prev, cur], axis=0)
        y = cur * w_ref[GDN_CONV - 1:GDN_CONV, :]
        for d in range(1, GDN_CONV):
            shifted = pltpu.roll(ext, d, 0)[SUBLANES:SUBLANES + chunk]
            y = y + shifted * w_ref[GDN_CONV - 1 - d:GDN_CONV - d, :]
        return y * _sigmoid(y)

    qc = conv_silu(q_ref, qp_ref, cwq_ref)
    kc = conv_silu(k_ref, kp_ref, cwk_ref)
    vc = conv_silu(v_ref, vp_ref, cwv_ref)

    raw = ba_ref[...]
    off = hg * hg_size
    b_raw = pltpu.roll(raw, (LANES - off) % LANES, 1)
    a_raw = pltpu.roll(raw, (2 * LANES - GDN_HEADS - off) % LANES, 1)
    beta_all = _sigmoid(b_raw)
    g_all = -jnp.exp(alog_ref[0]) * _softplus(a_raw + dtb_ref[0])
    rows = lax.broadcasted_iota(I32, g_all.shape, 0)
    gc_all = g_all
    span = 1
    while span < chunk:
        gc_all = gc_all + jnp.where(rows >= span, pltpu.roll(gc_all, span, 0), 0.0)
        span *= 2
    gc_t = gc_all.T
    eg_all = jnp.exp(gc_all)

    nst = GDN_STACK
    rows_g = nst * chunk
    ri = lax.broadcasted_iota(I32, (rows_g, rows_g), 0)
    ci = lax.broadcasted_iota(I32, (rows_g, rows_g), 1)
    same_head = (ri // chunk) == (ci // chunk)
    causal = same_head & (ri >= ci)
    strict = same_head & (ri > ci)

    def stack(x_all, width, h0):
        return jnp.concatenate([x_all[:, (h0 + s) * width:(h0 + s + 1) * width] for s in range(nst)], axis=0)

    z_all = z_ref[...]
    for h0 in range(0, hg_size, nst):
        q = _l2norm(stack(qc, dk, h0)) * dk ** -0.5
        k = _l2norm(stack(kc, dk, h0))
        v = stack(vc, dv, h0)
        beta = stack(beta_all, 1, h0)
        gcol = stack(gc_all, 1, h0)
        egc = stack(eg_all, 1, h0)
        grow = jnp.concatenate([gc_t[h0 + s:h0 + s + 1, :] for s in range(nst)], axis=1)
        decay = jnp.where(causal, jnp.exp(jnp.where(causal, gcol - grow, 0.0)), 0.0)
        kb = k * beta
        kbf = k.astype(BF16)
        p = _dot_nt(jnp.concatenate([kb, q], axis=0).astype(BF16), kbf)
        a_mat = jnp.where(strict, p[:rows_g] * decay, 0.0)
        qk = p[rows_g:] * decay
        t_mat = _unit_lower_inverse(a_mat, chunk).astype(BF16)
        uw = _dot(t_mat, jnp.concatenate([v * beta, kb * egc], axis=1).astype(BF16))
        u, w = uw[:, :dv], uw[:, dv:]
        qd = q * egc
        ws, qs = [], []
        for s in range(nst):
            r0, r1 = s * chunk, (s + 1) * chunk
            both = _dot(jnp.concatenate([w[r0:r1], qd[r0:r1]], axis=0).astype(BF16), s_s[h0 + s].astype(BF16))
            ws.append(both[:chunk])
            qs.append(both[chunk:])
        v_new = u - jnp.concatenate(ws, axis=0)
        v_new_bf = v_new.astype(BF16)
        o = jnp.concatenate(qs, axis=0) + _dot(qk.astype(BF16), v_new_bf)
        for s in range(nst):
            r0, r1 = s * chunk, (s + 1) * chunk
            g_last = gcol[r1 - 1:r1, :]
            k_dec = (k[r0:r1] * jnp.exp(g_last - gcol[r0:r1])).astype(BF16)
            s_s[h0 + s] = s_s[h0 + s] * jnp.exp(g_last) + _dot_tn(k_dec, v_new_bf[r0:r1])
        z = stack(z_all, dv, h0)
        on = o * lax.rsqrt(jnp.mean(o * o, -1, keepdims=True) + 1e-6) * nw_ref[...] * (z * _sigmoid(z))
        for s in range(nst):
            o_ref[:, (h0 + s) * dv:(h0 + s + 1) * dv] = on[s * chunk:(s + 1) * chunk].astype(o_ref.dtype)

    @pl.when(c == pl.num_programs(2) - 1)
    def _():
        sout_ref[0] = s_s[...]


def _gdn_prompt(proj, ba, conv_w, alog_g, dtb_g, norm_w, bsz, seq, col0):
    hgs, chunk = GDN_HEAD_GROUP, GDN_CHUNK
    n_hg = GDN_HEADS // hgs
    w = hgs * GDN_DK
    n_chunks = seq // chunk
    hw = GDN_HEADS * GDN_DK
    assert col0 % w == 0 and seq % chunk == 0
    qb, kb, vb, zb = (col0 // w + i * (hw // w) for i in range(4))
    cpt = chunk // SUBLANES

    def cur(base):
        return pl.BlockSpec((chunk, w), lambda b, hg, c: (b * n_chunks + c, base + hg))

    def prev(base):
        return pl.BlockSpec((SUBLANES, w),
                            lambda b, hg, c: (jnp.maximum((b * n_chunks + c) * cpt - 1, 0), base + hg))

    def cw(base):
        return pl.BlockSpec((GDN_CONV, w), lambda b, hg, c: (0, base + hg))

    return pl.pallas_call(
        functools.partial(_gdn_body, hg_size=hgs, chunk=chunk),
        grid=(bsz, n_hg, n_chunks),
        in_specs=[cur(qb), cur(kb), cur(vb), prev(qb), prev(kb), prev(vb), cur(zb),
                  pl.BlockSpec((chunk, LANES), lambda b, hg, c: (b * n_chunks + c, 0)),
                  cw(0), cw(hw // w), cw(2 * hw // w),
                  pl.BlockSpec((1, 1, LANES), lambda b, hg, c: (hg, 0, 0)),
                  pl.BlockSpec((1, 1, LANES), lambda b, hg, c: (hg, 0, 0)),
                  pl.BlockSpec((1, GDN_DV), lambda b, hg, c: (0, 0))],
        out_specs=[pl.BlockSpec((chunk, w), lambda b, hg, c: (b * n_chunks + c, hg)),
                   pl.BlockSpec((1, hgs, GDN_DK, GDN_DV), lambda b, hg, c: (b, hg, 0, 0))],
        out_shape=[jax.ShapeDtypeStruct((bsz * seq, hw), BF16),
                   jax.ShapeDtypeStruct((bsz, GDN_HEADS, GDN_DK, GDN_DV), F32)],
        scratch_shapes=[pltpu.VMEM((hgs, GDN_DK, GDN_DV), F32)],
        compiler_params=_cparams(("parallel", "parallel", "arbitrary")),
    )(proj, proj, proj, proj, proj, proj, proj, ba, conv_w, conv_w, conv_w, alog_g, dtb_g, norm_w)


def _gdn_step_body(new_ref, buf_ref, cw_ref, z_ref, b_ref, a_ref, alog_ref, dtb_ref, nw_ref, s_ref,
                   o_ref, sout_ref, cout_ref):
    nh = GDN_HEADS
    new = new_ref[0]
    y = new * cw_ref[GDN_CONV - 1]
    for j in range(GDN_CONV - 1):
        y = y + buf_ref[0, j] * cw_ref[j]
        if j > 0:
            cout_ref[0, j - 1] = buf_ref[0, j]
    cout_ref[0, GDN_CONV - 2] = new
    y = y * _sigmoid(y)
    q = _l2norm(y[0:nh]) * GDN_DK ** -0.5
    k = _l2norm(y[nh:2 * nh])
    v = y[2 * nh:3 * nh]
    beta = _sigmoid(b_ref[0])
    eg = jnp.exp(-jnp.exp(alog_ref[...]) * _softplus(a_ref[0] + dtb_ref[...]))
    q_t = q.T
    k_t = k.T
    z = z_ref[0]
    for h in range(nh):
        s_old = s_ref[0, h]
        kcol = k_t[:, h:h + 1]
        egh = eg[h:h + 1, :]
        sk = jnp.sum(kcol * s_old, axis=0, keepdims=True)
        v_new = beta[h:h + 1, :] * (v[h:h + 1, :] - egh * sk)
        s_new = s_old * egh + kcol * v_new
        sout_ref[0, h] = s_new
        o = jnp.sum(q_t[:, h:h + 1] * s_new, axis=0, keepdims=True)
        zh = z[h:h + 1, :]
        on = o * lax.rsqrt(jnp.mean(o * o, -1, keepdims=True) + 1e-6) * nw_ref[...] * (zh * _sigmoid(zh))
        o_ref[0, h:h + 1, :] = on


def _gdn_step(qkv_new, conv_buf, conv_w, z, b_in, a_in, a_log, dt_bias, norm_w, state):
    bs = qkv_new.shape[0]
    nh, nr = GDN_HEADS, 3 * GDN_HEADS
    per_b3 = lambda shape: pl.BlockSpec((1,) + shape, lambda b: (b, 0, 0))
    per_b4 = lambda shape: pl.BlockSpec((1,) + shape, lambda b: (b, 0, 0, 0))
    return pl.pallas_call(
        _gdn_step_body,
        grid=(bs,),
        in_specs=[per_b3((nr, GDN_DK)), per_b4((GDN_CONV - 1, nr, GDN_DK)),
                  pl.BlockSpec((GDN_CONV, nr, GDN_DK), lambda b: (0, 0, 0)),
                  per_b3((nh, GDN_DV)), per_b3((nh, 1)), per_b3((nh, 1)),
                  pl.BlockSpec((nh, 1), lambda b: (0, 0)), pl.BlockSpec((nh, 1), lambda b: (0, 0)),
                  pl.BlockSpec((1, GDN_DV), lambda b: (0, 0)),
                  per_b4((nh, GDN_DK, GDN_DV))],
        out_specs=[per_b3((nh, GDN_DV)), per_b4((nh, GDN_DK, GDN_DV)), per_b4((GDN_CONV - 1, nr, GDN_DK))],
        out_shape=[jax.ShapeDtypeStruct((bs, nh, GDN_DV), F32),
                   jax.ShapeDtypeStruct((bs, nh, GDN_DK, GDN_DV), F32),
                   jax.ShapeDtypeStruct((bs, GDN_CONV - 1, nr, GDN_DK), F32)],
        compiler_params=_cparams(("parallel",)),
    )(qkv_new, conv_buf, conv_w, z, b_in, a_in, a_log, dt_bias, norm_w, state)


def _lane_partner(x, lane, s):
    return jnp.where((lane & s) != 0, pltpu.roll(x, s, 1), pltpu.roll(x, LANES - s, 1))


def _group_reduce(x, lane, op):
    s = 1
    while s < GROUP_SIZE:
        x = op(x, _lane_partner(x, lane, s))
        s *= 2
    return x


def _router_body(hp_ref, rw_ref, rb_ref, idx_ref, w_ref, rank_ref, cnt_ref, carry_s):
    i = pl.program_id(0)

    @pl.when(i == 0)
    def _():
        carry_s[...] = jnp.zeros(carry_s.shape, F32)

    half = rw_ref.shape[0] // 2
    lo, hi = _unpack_bf16_pairs(_load_slabs(hp_ref, ROW_TILE, half // LANES))
    rw = rw_ref[...].astype(BF16)
    scores = _sigmoid(_dot(lo, rw[:half]) + _dot(hi, rw[half:]))
    choice = scores + rb_ref[...]
    tm = scores.shape[0]
    lane = lax.broadcasted_iota(I32, (tm, LANES), 1)
    grp = lane // GROUP_SIZE
    big = jnp.int32(2 * LANES)

    m1 = _group_reduce(choice, lane, jnp.maximum)
    first = _group_reduce(jnp.where(choice == m1, lane, big), lane, jnp.minimum)
    m2 = _group_reduce(jnp.where(lane == first, NEG_INF, choice), lane, jnp.maximum)
    gs = m1 + m2
    beaten = jnp.zeros((tm, LANES), I32)
    for d in range(1, N_GROUPS):
        other = pltpu.roll(gs, d * GROUP_SIZE, 1)
        other_grp = pltpu.roll(grp, d * GROUP_SIZE, 1)
        beats = (other > gs) | ((other == gs) & (other_grp < grp))
        beaten = beaten + beats.astype(I32)
    masked = jnp.where(beaten < TOPK_GROUPS, choice, NEG_INF)

    idx_out = jnp.zeros((tm, LANES), I32)
    w_out = jnp.zeros((tm, LANES), F32)
    onehot = jnp.zeros((tm, LANES), F32)
    sels = []
    for k in range(TOP_K):
        m = jnp.max(masked, axis=-1, keepdims=True)
        idx = jnp.min(jnp.where(masked == m, lane, big), axis=-1, keepdims=True)
        sel = lane == idx
        sels.append(sel)
        wk = jnp.sum(jnp.where(sel, scores, 0.0), axis=-1, keepdims=True)
        idx_out = jnp.where(lane == k, idx, idx_out)
        w_out = jnp.where(lane == k, wk, w_out)
        onehot = jnp.where(sel, 1.0, onehot)
        masked = jnp.where(sel, NEG_INF, masked)
    w_out = w_out / jnp.sum(w_out, axis=-1, keepdims=True) * ROUTED_SCALE

    ri = lax.broadcasted_iota(I32, (tm, tm), 0)
    ci = lax.broadcasted_iota(I32, (tm, tm), 1)
    before = _dot((ri > ci).astype(BF16), onehot.astype(BF16)) + carry_s[...]
    rank_out = jnp.zeros((tm, LANES), F32)
    for k in range(TOP_K):
        rk = jnp.sum(jnp.where(sels[k], before, 0.0), axis=-1, keepdims=True)
        rank_out = jnp.where(lane == k, rk, rank_out)
    carry_s[...] = carry_s[...] + jnp.sum(onehot, axis=0, keepdims=True)

    idx_ref[...] = idx_out
    w_ref[...] = w_out
    rank_ref[...] = rank_out.astype(I32)
    cnt_ref[...] = jnp.broadcast_to(carry_s[...], cnt_ref.shape).astype(I32)


def _router(h_slabs, router_w, router_bias):
    d = router_w.shape[0]
    pitch = d // 2 // LANES
    t = h_slabs.shape[0] // pitch
    row = pl.BlockSpec((ROW_TILE, LANES), lambda i: (i, 0))
    return pl.pallas_call(
        _router_body,
        grid=(t // ROW_TILE,),
        in_specs=[pl.BlockSpec((ROW_TILE * pitch, LANES), lambda i: (i, 0)),
                  pl.BlockSpec((d, N_EXPERTS), lambda i: (0, 0)),
                  pl.BlockSpec((1, N_EXPERTS), lambda i: (0, 0))],
        out_specs=[row, row, row, pl.BlockSpec((SUBLANES, LANES), lambda i: (0, 0))],
        out_shape=[jax.ShapeDtypeStruct((t, LANES), I32), jax.ShapeDtypeStruct((t, LANES), F32),
                   jax.ShapeDtypeStruct((t, LANES), I32), jax.ShapeDtypeStruct((SUBLANES, LANES), I32)],
        scratch_shapes=[pltpu.VMEM((1, LANES), F32)],
        compiler_params=_cparams(("arbitrary",)),
    )(h_slabs, router_w, router_bias)


def _gather_body(used_ref, idx_ref, tab_ref, out_ref, sem, *, rows, n, src_pitch):
    def slab_copy(src, dst):
        return pltpu.make_async_copy(tab_ref.at[pl.ds(src * src_pitch, n)],
                                     out_ref.at[pl.ds(pl.multiple_of(dst * n, n), n)], sem)

    def issue(pair, carry):
        for lane in range(2):
            r = 2 * pair + lane
            slab_copy(idx_ref[0, 0, r], r).start(priority=lane)
        return carry

    def drain(r, carry):
        slab_copy(0, 0).wait()
        return carry

    in_use = pl.program_id(0) * rows < used_ref[0]

    @pl.when(in_use)
    def _():
        lax.fori_loop(0, rows // 2, issue, 0)
        lax.fori_loop(0, rows, drain, 0)

    @pl.when(jnp.logical_not(in_use))
    def _():
        out_ref[...] = jnp.zeros(out_ref.shape, out_ref.dtype)


def _slab_gather(table, idx, n, n_used, src_pitch=None):
    src_pitch = n if src_pitch is None else src_pitch
    count = idx.shape[0]
    rows = GATHER_ROWS
    assert count % rows == 0 and rows % 2 == 0
    return pl.pallas_call(
        functools.partial(_gather_body, rows=rows, n=n, src_pitch=src_pitch),
        grid_spec=pltpu.PrefetchScalarGridSpec(
            num_scalar_prefetch=1,
            grid=(count // rows,),
            in_specs=[pl.BlockSpec((1, 1, rows), lambda i, used: (i, 0, 0), memory_space=pltpu.SMEM),
                      pl.BlockSpec(memory_space=pl.ANY)],
            out_specs=pl.BlockSpec((rows * n, LANES), lambda i, used: (i, 0)),
            scratch_shapes=[pltpu.SemaphoreType.DMA(())],
        ),
        out_shape=jax.ShapeDtypeStruct((count * n, LANES), table.dtype),
        compiler_params=_cparams(("arbitrary",)),
    )(jnp.asarray(n_used, I32).reshape(1), idx.reshape(count // rows, 1, rows), table)


def _ffn_up_body(be_ref, bv_ref, x_ref, wg_ref, wu_ref, o_ref):
    i = pl.program_id(0)

    @pl.when(bv_ref[i] > 0)
    def _():
        half = wg_ref.shape[0] // 2
        lo, hi = _unpack_bf16_pairs(_load_slabs(x_ref, o_ref.shape[0], half // LANES))
        wg = wg_ref[...].astype(BF16)
        wu = wu_ref[...].astype(BF16)
        g = _dot(lo, wg[:half]) + _dot(hi, wg[half:])
        u = _dot(lo, wu[:half]) + _dot(hi, wu[half:])
        o_ref[...] = (g * _sigmoid(g) * u).astype(o_ref.dtype)

    @pl.when(bv_ref[i] == 0)
    def _():
        o_ref[...] = jnp.zeros(o_ref.shape, o_ref.dtype)


def _ffn_down_body(be_ref, bv_ref, a_ref, wd_ref, o_ref, *, slab_pitch, k_chunk):
    i = pl.program_id(1)

    @pl.when(bv_ref[i] > 0)
    def _():
        a = a_ref[...]
        acc = None
        for c0 in range(0, a.shape[1], k_chunk):
            part = _dot(a[:, c0:c0 + k_chunk], wd_ref[c0:c0 + k_chunk, :].astype(BF16))
            acc = part if acc is None else acc + part
        if slab_pitch:
            _store_slabs(o_ref, acc, slab_pitch)
        else:
            o_ref[...] = acc

    @pl.when(bv_ref[i] == 0)
    def _():
        o_ref[...] = jnp.zeros(o_ref.shape, o_ref.dtype)


def _ffn_up(x_slabs, blocks, w_gate, w_up, layer, tm, tf):
    block_e, block_valid, block_row = blocks
    d = w_gate.shape[-2]
    ff = w_gate.shape[-1]
    pitch = d // 2 // LANES
    n_rows = x_slabs.shape[0] // pitch
    nb = block_e.shape[0]
    nj = ff // tf

    def w_index(i, j, be, bv):
        return layer, be[i], 0, jnp.where(bv[i] > 0, j, nj - 1)

    up = pl.pallas_call(
        _ffn_up_body,
        grid_spec=pltpu.PrefetchScalarGridSpec(
            num_scalar_prefetch=2,
            grid=(nb, nj),
            in_specs=[pl.BlockSpec((tm * pitch, LANES), lambda i, j, be, bv: (bv[nb + i], 0)),
                      pl.BlockSpec((None, None, d, tf), w_index),
                      pl.BlockSpec((None, None, d, tf), w_index)],
            out_specs=pl.BlockSpec((tm, tf), lambda i, j, be, bv: (i, j)),
        ),
        out_shape=jax.ShapeDtypeStruct((n_rows, ff), BF16),
        compiler_params=_cparams(("arbitrary", "arbitrary")),
    )
    return up(block_e, jnp.concatenate([block_valid, block_row]), x_slabs, w_gate, w_up)


def _ffn_down(act, blocks, w_down, layer, tm, tn, slab_pitch=0):
    block_e, block_valid, block_row = blocks
    n_rows, ff = act.shape
    d = w_down.shape[-1]
    nb = block_e.shape[0]
    assert not slab_pitch or tn == d
    bv = jnp.concatenate([block_valid, block_row])
    if slab_pitch:
        out_spec = pl.BlockSpec((tm * slab_pitch, LANES), lambda j, i, be, bv: (i, 0))
        out_shape = jax.ShapeDtypeStruct((n_rows * slab_pitch, LANES), F32)
    else:
        out_spec = pl.BlockSpec((tm, tn), lambda j, i, be, bv: (i, j))
        out_shape = jax.ShapeDtypeStruct((n_rows, d), F32)
    down = pl.pallas_call(
        functools.partial(_ffn_down_body, slab_pitch=slab_pitch, k_chunk=min(ff, 256)),
        grid_spec=pltpu.PrefetchScalarGridSpec(
            num_scalar_prefetch=2,
            grid=(d // tn, nb),
            in_specs=[pl.BlockSpec((tm, ff), lambda j, i, be, bv: (bv[nb + i], 0)),
                      pl.BlockSpec((None, None, ff, tn), lambda j, i, be, bv: (layer, be[i], 0, j))],
            out_specs=out_spec,
        ),
        out_shape=out_shape,
        compiler_params=_cparams(("arbitrary", "arbitrary")),
    )
    return down(block_e, bv, act, w_down)


def _moe(h_slabs, p, layer, t):
    d = p["router_w"].shape[0]
    top_idx, top_w, rank, counts = _router(h_slabs, p["router_w"], p["router_bias"])
    tm = MOE_ROW_BLOCK
    counts = counts[0]
    padded = (counts + tm - 1) // tm * tm
    pad_end = jnp.cumsum(padded)
    pad_start = pad_end - padded
    experts = jnp.arange(N_EXPERTS, dtype=I32)
    start_of = jnp.sum(jnp.where(top_idx[:, :TOP_K, None] == experts, pad_start, 0), axis=-1)
    pos = (start_of + rank[:, :TOP_K]).reshape(-1)
    n_blocks = (t * TOP_K) // tm + N_EXPERTS
    while (n_blocks * tm) % GATHER_ROWS:
        n_blocks += 1
    n_slots = n_blocks * tm
    slot_tok = jnp.zeros((n_slots,), I32).at[pos].set(jnp.repeat(jnp.arange(t, dtype=I32), TOP_K))
    n_used_rows = pad_end[-1]

    def blocks_of(rows_per_block):
        nb = n_blocks * (tm // rows_per_block)
        starts = jnp.arange(nb, dtype=I32) * rows_per_block
        blk = jnp.minimum(jnp.arange(nb, dtype=I32), n_used_rows // rows_per_block - 1)
        e = jnp.minimum(jnp.searchsorted(pad_end, blk * rows_per_block, side="right"), N_EXPERTS - 1).astype(I32)
        real_end = (pad_start + counts)[e]
        valid = ((starts < n_used_rows) & (starts < real_end)).astype(I32)
        return e, valid, blk

    x_sorted = _slab_gather(h_slabs, slot_tok, d // 2 // LANES, n_used_rows)
    y_pitch = d // LANES + Y_SLAB_PAD
    act = _ffn_up(x_sorted, blocks_of(tm), p["exp_gate"], p["exp_up"], layer, tm=tm,
                  tf=min(256, p["exp_gate"].shape[-1]))
    dtm = tm // 2
    y_sorted = _ffn_down(act, blocks_of(dtm), p["exp_down"], layer, tm=dtm, tn=d, slab_pitch=y_pitch)
    y_tok = _slab_gather(y_sorted, pos, d // LANES, t * TOP_K, src_pitch=y_pitch)

    stm = _pick_tm(t, 1024)
    nsb = t // stm
    one = jnp.ones((nsb,), I32)
    sblocks = (0 * one, one, jnp.arange(nsb, dtype=I32))
    lead = lambda w: w.reshape((w.shape[0], 1) + w.shape[1:])
    sact = _ffn_up(h_slabs, sblocks, lead(p["sh_gate"]), lead(p["sh_up"]), layer, tm=stm, tf=256)
    shared = _ffn_down(sact, sblocks, lead(p["sh_down"]), layer, tm=stm, tn=1024)
    return y_tok, top_w, shared


def kernel(x_prompt, x_sample, cache_k, cache_v, state_gdn, state_conv, page_table, c_prompt, c_sample, rel_bias, ada_w, ada_b, w_in, lam_q1, lam_k1, lam_q2, lam_k2, subln_w, conv_w, a_log, dt_bias, gdn_norm_w, w_branch, w_o, ln1_g, ln1_b, router_w, router_bias, exp_gate, exp_up, exp_down, sh_gate, sh_up, sh_down, ln2_g, ln2_b):
    bsz, seq, d = x_prompt.shape
    bs = x_sample.shape[0]
    depth = ada_w.shape[0]
    tp = bsz * seq
    t = tp + bs
    n_pages = page_table.shape[1]
    page = cache_k.shape[2]
    past = n_pages * page
    alpha = (2 * depth) ** 0.25
    assert bs == ROW_TILE and seq % ATTN_BLOCK == 0 and x_sample.shape[1] == 1

    x = jnp.concatenate([x_prompt.reshape(tp, d), x_sample.reshape(bs, d)], axis=0)
    c_all = jnp.concatenate([c_prompt, c_sample, jnp.zeros((-(bsz + bs) % SUBLANES, d), F32)], axis=0)

    q_w = N_ATTN_HEADS * 2 * ATTN_HD
    kv_w = N_KV_HEADS * 2 * ATTN_HD
    hw = GDN_HEADS * GDN_DK
    gdn0 = q_w + 2 * kv_w
    main_w = gdn0 + 4 * hw
    ba_w = 2 * GDN_HEADS

    blk = ATTN_BLOCK
    rel_line = jnp.arange(-(blk - 1), 2 * blk)
    line = jnp.where(rel_line >= 0, rel_bias[_t5_bucket(rel_line)].astype(F32).T, NEG_INF)
    bias_tiles = jnp.stack([_toeplitz(line[:, :2 * blk - 1]), _toeplitz(line[:, blk:])], axis=1)
    bias_far = rel_bias[N_BUCKETS - 1].astype(F32)
    rel_dec = past - jnp.arange(past)
    dec = rel_bias[_t5_bucket(rel_dec)].astype(F32).reshape(n_pages, page, N_KV_HEADS, ATTN_GROUP)
    dec = jnp.transpose(dec, (0, 2, 3, 1))[:, :, None, :, :, None]
    same_kv = jnp.eye(N_KV_HEADS, dtype=bool)[None, :, None, None, None, :]
    sbias = jnp.where(same_kv, jnp.broadcast_to(dec, (n_pages, N_KV_HEADS, 2, ATTN_GROUP, page, N_KV_HEADS)),
                      NEG_INF).reshape(n_pages, N_KV_HEADS * 2 * ATTN_GROUP, page * N_KV_HEADS)
    self_b = rel_bias[0].astype(F32).reshape(N_KV_HEADS, 1, ATTN_GROUP)
    sbias_self = jnp.broadcast_to(jnp.broadcast_to(self_b, (N_KV_HEADS, 2, ATTN_GROUP)).reshape(
        N_KV_HEADS * 2 * ATTN_GROUP, 1), (N_KV_HEADS * 2 * ATTN_GROUP, LANES))

    y = x
    outs = {n: [] for n in ("kp", "vp", "sp", "cp", "ks", "vs", "ss", "cs")}
    for l in range(depth):
        lam_init = 0.8 - 0.6 * math.exp(-0.3 * l)
        p = {"router_w": router_w[l], "router_bias": router_bias[l].reshape(1, -1), "exp_gate": exp_gate,
             "exp_up": exp_up, "exp_down": exp_down, "sh_gate": sh_gate, "sh_up": sh_up, "sh_down": sh_down}

        mod = _mm(c_all, ada_w[l], tm=c_all.shape[0], tn=512, n_out=6 * d, bias=ada_b[l].reshape(1, -1),
                  a_silu=True)
        mod_p = mod[:bsz].reshape(bsz, 1, 6 * d)
        mod_s = mod[bsz:bsz + bs]

        h = _modulate(y, mod_p, mod_s, 1, 0, seq)
        tm_big = _pick_tm(t, 1664)
        proj = _mm(h, w_in[l], tm=tm_big, tn=256, n_out=main_w)
        w_tail = w_in[l][:, main_w:]
        w_ba = jnp.pad(w_tail[:, :ba_w], ((0, 0), (0, LANES - ba_w)))
        ba = _mm(h, w_ba, tm=tm_big, tn=LANES, n_out=LANES)
        gates = _mm(h, w_tail[:, ba_w:], tm=tm_big, tn=256, n_out=2 * d)

        lam_vecs = jnp.stack([lam_q1[l], lam_k1[l], lam_q2[l], lam_k2[l]]).astype(F32)
        sub_w = subln_w[l].reshape(1, -1).astype(F32)

        o_a_p = _flash_attention(proj, bias_tiles, bias_far, lam_vecs, sub_w, bsz, seq, lam_init)
        srow = proj[tp:]
        q_s = srow[:, :q_w].reshape(bs, N_KV_HEADS, ATTN_GROUP, 2, ATTN_HD)
        zeros = jnp.zeros_like(q_s[:, :, :, 0])
        qbd = jnp.stack([jnp.concatenate([q_s[:, :, :, 0], zeros], -1),
                         jnp.concatenate([zeros, q_s[:, :, :, 1]], -1)], axis=2)
        qbd = qbd.reshape(bs, N_KV_HEADS * 2 * ATTN_GROUP, 2 * ATTN_HD)
        k_s = srow[:, q_w:q_w + kv_w]
        v_s = srow[:, q_w + kv_w:gdn0]
        o_a_s = _paged_attention(page_table, qbd, k_s.reshape(bs, N_KV_HEADS, 2 * ATTN_HD),
                                 v_s.reshape(bs, N_KV_HEADS, ATTN_VD), cache_k, cache_v, l, sbias,
                                 sbias_self, lam_vecs, sub_w, lam_init)
        o_a = jnp.concatenate([o_a_p, o_a_s.reshape(bs, -1).astype(BF16)], axis=0)

        n_hg = GDN_HEADS // GDN_HEAD_GROUP
        group_lanes = lambda vec: jnp.pad(vec.astype(F32).reshape(n_hg, 1, GDN_HEAD_GROUP),
                                          ((0, 0), (0, 0), (0, LANES - GDN_HEAD_GROUP)))
        nw = gdn_norm_w[l].reshape(1, -1).astype(F32)
        o_g_p, s_p = _gdn_prompt(proj, ba, conv_w[l], group_lanes(a_log[l]), group_lanes(dt_bias[l]), nw,
                                 bsz, seq, gdn0)
        nr = 3 * GDN_HEADS
        o_g_s, s_s, c_s = _gdn_step(
            srow[:, gdn0:gdn0 + 3 * hw].reshape(bs, nr, GDN_DK),
            state_conv[l].reshape(bs, GDN_CONV - 1, nr, GDN_DK),
            conv_w[l].reshape(GDN_CONV, nr, GDN_DK),
            srow[:, gdn0 + 3 * hw:main_w].reshape(bs, GDN_HEADS, GDN_DV),
            ba[tp:, :GDN_HEADS].reshape(bs, GDN_HEADS, 1), ba[tp:, GDN_HEADS:ba_w].reshape(bs, GDN_HEADS, 1),
            a_log[l].reshape(-1, 1).astype(F32), dt_bias[l].reshape(-1, 1).astype(F32), nw, state_gdn[l])
        o_g = jnp.concatenate([o_g_p, o_g_s.reshape(bs, -1).astype(BF16)], axis=0)

        tm_mid = _pick_tm(t, 1024)
        m_a = _mm(o_a, w_branch[l], tm=tm_mid, tn=512, n_out=d, gate=gates)
        merged = _mm(o_g, w_branch[l], tm=tm_mid, tn=512, n_out=d, w_row_blk=1, gate=gates,
                     gate_col_blk=d // 512, prev=m_a, out_dtype=BF16)
        attn_out = _mm(merged, w_o[l], tm=tm_mid, tn=512, n_out=d)
        x1, h2p = _ln1(y, attn_out, mod_p, mod_s, ln1_g[l].reshape(1, -1), ln1_b[l].reshape(1, -1), seq, alpha)

        y_tok, top_w, shared = _moe(h2p, p, l, t)
        y = _final(x1, y_tok, top_w, shared, mod_p, mod_s, ln2_g[l].reshape(1, -1), ln2_b[l].reshape(1, -1),
                   seq, alpha)

        kcols = proj[:, q_w:q_w + kv_w]
        vcols = proj[:, q_w + kv_w:gdn0]
        outs["kp"].append(kcols[:tp].reshape(bsz, seq, N_KV_HEADS, 2 * ATTN_HD))
        outs["vp"].append(vcols[:tp].reshape(bsz, seq, N_KV_HEADS, ATTN_VD))
        outs["sp"].append(s_p)
        tail = jnp.stack([proj[(b + 1) * seq - (GDN_CONV - 1):(b + 1) * seq, gdn0:gdn0 + 3 * hw]
                          for b in range(bsz)])
        outs["cp"].append(tail)
        outs["ks"].append(kcols[tp:].reshape(bs, 1, N_KV_HEADS, 2 * ATTN_HD))
        outs["vs"].append(vcols[tp:].reshape(bs, 1, N_KV_HEADS, ATTN_VD))
        outs["ss"].append(s_s)
        outs["cs"].append(c_s.reshape(bs, GDN_CONV - 1, 3 * hw))

    st = lambda n: jnp.stack(outs[n])
    return (y[:tp].reshape(bsz, seq, d), y[tp:].reshape(bs, 1, d), st("kp"), st("vp"), st("sp"), st("cp"),
            st("ks"), st("vs"), st("ss"), st("cs"))
```

```python
import functools
import math

import jax
import jax.numpy as jnp
from jax import lax
from jax.experimental import pallas as pl
from jax.experimental.pallas import tpu as pltpu

F32 = jnp.float32
BF16 = jnp.bfloat16
U32 = jnp.uint32
I32 = jnp.int32

N_ATTN_HEADS = 16
N_KV_HEADS = 4
ATTN_GROUP = N_ATTN_HEADS // N_KV_HEADS
ATTN_HD = 128
ATTN_VD = 256
N_BUCKETS = 32
MAX_DISTANCE = 128
GDN_HEADS = 32
GDN_DK = 128
GDN_DV = 128
GDN_CONV = 4
GDN_CHUNK = 64
N_EXPERTS = 128
TOP_K = 8
N_GROUPS = 8
GROUP_SIZE = N_EXPERTS // N_GROUPS
TOPK_GROUPS = 4
ROUTED_SCALE = 2.5

LANES = 128
SUBLANES = 8
VMEM_LIMIT = 56 * 1024 * 1024

ROW_TILE = 128
FINAL_TILE = 64
ATTN_BLOCK = 256
GDN_HEAD_GROUP = 16
GDN_STACK = 4
MOE_ROW_BLOCK = 640
GATHER_ROWS = 256
Y_SLAB_PAD = 4

NEG_INF = float("-inf")


def _cparams(sem, vmem=VMEM_LIMIT):
    return pltpu.CompilerParams(dimension_semantics=sem, vmem_limit_bytes=vmem)


def _sigmoid(x):
    return jax.nn.sigmoid(x)


def _dot(a, b):
    return jnp.dot(a, b, preferred_element_type=F32)


def _dot_nt(a, b):
    return lax.dot_general(a, b, (((1,), (1,)), ((), ())), preferred_element_type=F32)


def _dot_tn(a, b):
    return lax.dot_general(a, b, (((0,), (0,)), ((), ())), preferred_element_type=F32)


def _mm_body(*refs, a_silu, has_bias, has_gate, has_prev):
    it = iter(refs)
    a_ref = next(it)
    w_ref = next(it)
    bias_ref = next(it) if has_bias else None
    gate_ref = next(it) if has_gate else None
    prev_ref = next(it) if has_prev else None
    o_ref = next(it)
    a = a_ref[...]
    if a_silu:
        a = a.astype(F32)
        a = a * _sigmoid(a)
    acc = _dot(a.astype(BF16), w_ref[...].astype(BF16))
    if has_bias:
        acc = acc + bias_ref[...]
    if has_gate:
        acc = _sigmoid(gate_ref[...]) * acc
    if has_prev:
        acc = acc + prev_ref[...]
    o_ref[...] = acc.astype(o_ref.dtype)


def _mm(a, w, *, tm, tn, n_out, w_row_blk=0, w_col_blk=0, bias=None, gate=None, gate_col_blk=0,
        prev=None, a_silu=False, out_dtype=F32):
    m, k = a.shape
    assert m % tm == 0 and n_out % tn == 0
    in_specs = [pl.BlockSpec((tm, k), lambda i, j: (i, 0)),
                pl.BlockSpec((k, tn), lambda i, j: (w_row_blk, j + w_col_blk))]
    args = [a, w]
    if bias is not None:
        in_specs.append(pl.BlockSpec((1, tn), lambda i, j: (0, j)))
        args.append(bias)
    if gate is not None:
        in_specs.append(pl.BlockSpec((tm, tn), lambda i, j: (i, j + gate_col_blk)))
        args.append(gate)
    if prev is not None:
        in_specs.append(pl.BlockSpec((tm, tn), lambda i, j: (i, j)))
        args.append(prev)
    body = functools.partial(_mm_body, a_silu=a_silu, has_bias=bias is not None,
                             has_gate=gate is not None, has_prev=prev is not None)
    return pl.pallas_call(
        body,
        grid=(m // tm, n_out // tn),
        in_specs=in_specs,
        out_specs=pl.BlockSpec((tm, tn), lambda i, j: (i, j)),
        out_shape=jax.ShapeDtypeStruct((m, n_out), out_dtype),
        compiler_params=_cparams(("parallel", "arbitrary")),
    )(*args)


def _pick_tm(m, cap):
    best = None
    for t in range(LANES, cap + 1, LANES):
        if m % t == 0:
            best = t
    assert best is not None
    return best


def _row_mod(i, n_prompt_tiles, p_ref, s_ref):
    return jnp.where(i >= n_prompt_tiles, s_ref[...], p_ref[0])


def _mod_specs(d, chunk, tiles_per_seq, n_prompt_tiles, tile=ROW_TILE):
    p_spec = pl.BlockSpec((1, 1, d), lambda i: (jnp.minimum(i, n_prompt_tiles - 1) // tiles_per_seq, 0, chunk))
    s_spec = pl.BlockSpec((tile, d), lambda i: (jnp.maximum(i - n_prompt_tiles, 0), chunk))
    return p_spec, s_spec


def _modulate_body(x_ref, scp_ref, scs_ref, shp_ref, shs_ref, o_ref, *, n_prompt_tiles):
    i = pl.program_id(0)
    sc = _row_mod(i, n_prompt_tiles, scp_ref, scs_ref)
    sh = _row_mod(i, n_prompt_tiles, shp_ref, shs_ref)
    o_ref[...] = (x_ref[...] * (1.0 + sc) + sh).astype(o_ref.dtype)


def _modulate(x, mod_p, mod_s, sc_chunk, sh_chunk, seq):
    t, d = x.shape
    n_prompt_tiles = mod_p.shape[0] * seq // ROW_TILE
    tps = seq // ROW_TILE
    scp, scs = _mod_specs(d, sc_chunk, tps, n_prompt_tiles)
    shp, shs = _mod_specs(d, sh_chunk, tps, n_prompt_tiles)
    return pl.pallas_call(
        functools.partial(_modulate_body, n_prompt_tiles=n_prompt_tiles),
        grid=(t // ROW_TILE,),
        in_specs=[pl.BlockSpec((ROW_TILE, d), lambda i: (i, 0)), scp, scs, shp, shs],
        out_specs=pl.BlockSpec((ROW_TILE, d), lambda i: (i, 0)),
        out_shape=jax.ShapeDtypeStruct((t, d), BF16),
        compiler_params=_cparams(("parallel",)),
    )(x, mod_p, mod_s, mod_p, mod_s)


def _layer_norm_rows(v, g, b):
    mu = jnp.mean(v, -1, keepdims=True)
    var = jnp.mean(jnp.square(v - mu), -1, keepdims=True)
    return (v - mu) * lax.rsqrt(var + 1e-5) * g + b


def _pack_bf16_pairs(h):
    half = h.shape[1] // 2
    bits = lax.bitcast_convert_type(h.astype(BF16).astype(F32), U32)
    return (bits[:, :half] >> 16) | (bits[:, half:] & jnp.uint32(0xFFFF0000))


def _unpack_bf16_pairs(p):
    lo = lax.bitcast_convert_type(p << 16, F32).astype(BF16)
    hi = lax.bitcast_convert_type(p & jnp.uint32(0xFFFF0000), F32).astype(BF16)
    return lo, hi


def _store_slabs(o_ref, v, pitch=None):
    rows, n = v.shape[0], v.shape[1] // LANES
    pitch = n if pitch is None else pitch
    for s in range(n):
        o_ref[pl.ds(s, rows, stride=pitch), :] = v[:, s * LANES:(s + 1) * LANES]
    for s in range(n, pitch):
        o_ref[pl.ds(s, rows, stride=pitch), :] = jnp.zeros((rows, LANES), v.dtype)


def _load_slabs(ref, rows, n, pitch=None, base=0):
    pitch = n if pitch is None else pitch
    return jnp.concatenate([ref[pl.ds(base + s, rows, stride=pitch), :] for s in range(n)], axis=1)


def _ln1_body(x_ref, y_ref, gp_ref, gs_ref, scp_ref, scs_ref, shp_ref, shs_ref, lg_ref, lb_ref,
              x1_ref, hp_ref, *, n_prompt_tiles, alpha):
    i = pl.program_id(0)
    g1 = _row_mod(i, n_prompt_tiles, gp_ref, gs_ref)
    sc = _row_mod(i, n_prompt_tiles, scp_ref, scs_ref)
    sh = _row_mod(i, n_prompt_tiles, shp_ref, shs_ref)
    x1 = _layer_norm_rows(alpha * x_ref[...] + g1 * y_ref[...], lg_ref[...], lb_ref[...])
    x1_ref[...] = x1
    _store_slabs(hp_ref, _pack_bf16_pairs(x1 * (1.0 + sc) + sh))


def _ln1(x, y, mod_p, mod_s, ln_g, ln_b, seq, alpha):
    t, d = x.shape
    n_prompt_tiles = mod_p.shape[0] * seq // ROW_TILE
    tps = seq // ROW_TILE
    gp, gs = _mod_specs(d, 2, tps, n_prompt_tiles)
    shp, shs = _mod_specs(d, 3, tps, n_prompt_tiles)
    scp, scs = _mod_specs(d, 4, tps, n_prompt_tiles)
    row = pl.BlockSpec((ROW_TILE, d), lambda i: (i, 0))
    vec = pl.BlockSpec((1, d), lambda i: (0, 0))
    return pl.pallas_call(
        functools.partial(_ln1_body, n_prompt_tiles=n_prompt_tiles, alpha=alpha),
        grid=(t // ROW_TILE,),
        in_specs=[row, row, gp, gs, scp, scs, shp, shs, vec, vec],
        out_specs=[row, pl.BlockSpec((ROW_TILE * (d // 2 // LANES), LANES), lambda i: (i, 0))],
        out_shape=[jax.ShapeDtypeStruct((t, d), F32), jax.ShapeDtypeStruct((t * (d // 2 // LANES), LANES), U32)],
        compiler_params=_cparams(("parallel",)),
    )(x, y, mod_p, mod_s, mod_p, mod_s, mod_p, mod_s, ln_g, ln_b)


def _final_body(x_ref, yt_ref, tw_ref, sh_ref, gp_ref, gs_ref, lg_ref, lb_ref, o_ref, *,
                n_prompt_tiles, alpha, d):
    i = pl.program_id(0)
    g2 = _row_mod(i, n_prompt_tiles, gp_ref, gs_ref)
    f = sh_ref[...]
    tw = tw_ref[...]
    rows, n = f.shape[0], d // LANES
    for k in range(TOP_K):
        f = f + _load_slabs(yt_ref, rows, n, pitch=TOP_K * n, base=k * n) * tw[:, k:k + 1]
    o_ref[...] = _layer_norm_rows(alpha * x_ref[...] + g2 * f, lg_ref[...], lb_ref[...])


def _final(x1, y_tok, top_w, shared, mod_p, mod_s, ln_g, ln_b, seq, alpha):
    t, d = x1.shape
    tile = FINAL_TILE
    n_prompt_tiles = mod_p.shape[0] * seq // tile
    gp, gs = _mod_specs(d, 5, seq // tile, n_prompt_tiles, tile)
    row = pl.BlockSpec((tile, d), lambda i: (i, 0))
    vec = pl.BlockSpec((1, d), lambda i: (0, 0))
    return pl.pallas_call(
        functools.partial(_final_body, n_prompt_tiles=n_prompt_tiles, alpha=alpha, d=d),
        grid=(t // tile,),
        in_specs=[row, pl.BlockSpec((tile * TOP_K * (d // LANES), LANES), lambda i: (i, 0)),
                  pl.BlockSpec((tile, LANES), lambda i: (i, 0)), row, gp, gs, vec, vec],
        out_specs=row,
        out_shape=jax.ShapeDtypeStruct((t, d), F32),
        compiler_params=_cparams(("parallel",)),
    )(x1, y_tok, top_w, shared, mod_p, mod_s, ln_g, ln_b)


def _t5_bucket(rel):
    n = jnp.maximum(rel, 0)
    max_exact = N_BUCKETS // 2
    nf = jnp.maximum(n, 1).astype(F32)
    large = max_exact + (jnp.log(nf / max_exact) / math.log(MAX_DISTANCE / max_exact)
                         * (N_BUCKETS - max_exact)).astype(I32)
    return jnp.where(n < max_exact, n, jnp.minimum(large, N_BUCKETS - 1))


def _toeplitz(a):
    n = (a.shape[-1] + 1) // 2
    zero = jnp.zeros(a.shape[:-1] + (1,), a.dtype)
    u = jnp.concatenate([a[..., n - 1::-1], zero, a[..., :n - 1:-1]], axis=-1)
    skew = jnp.tile(u, n)[..., :n * (2 * n - 1)].reshape(a.shape[:-1] + (n, 2 * n - 1))
    return skew[..., :n]


def _lambda(lam_ref, lam_init):
    lam = lam_ref[...]
    s1 = jnp.sum(lam[0:1] * lam[1:2], axis=-1, keepdims=True)
    s2 = jnp.sum(lam[2:3] * lam[3:4], axis=-1, keepdims=True)
    return jnp.exp(s1) - jnp.exp(s2) + lam_init


def _sub_norm(o, w, lam_init):
    return o * lax.rsqrt(jnp.mean(o * o, -1, keepdims=True) + 1e-5) * w * (1.0 - lam_init)


def _online_update(idx, s, v, m_s, l_s, acc_s):
    m_old = m_s[idx]
    m_new = jnp.maximum(m_old, jnp.max(s, axis=-1, keepdims=True))
    p = jnp.exp(s - m_new)
    corr = jnp.exp(m_old - m_new)
    l_s[idx] = corr * l_s[idx] + jnp.sum(p, axis=-1, keepdims=True)
    acc_s[idx] = corr * acc_s[idx] + _dot(p.astype(BF16), v)
    m_s[idx] = m_new


def _flash_body(qt_ref, kt_ref, far_ref, q_ref, k_ref, v_ref, bt_ref, lam_ref, sub_ref, o_ref, m_s, l_s, acc_s, *,
                scale, lam_init):
    kv = pl.program_id(1)
    qi = qt_ref[pl.program_id(2)]
    ki = kt_ref[pl.program_id(2)]
    hd, vd, grp = ATTN_HD, ATTN_VD, ATTN_GROUP

    @pl.when(ki == 0)
    def _():
        m_s[...] = jnp.full(m_s.shape, NEG_INF, F32)
        l_s[...] = jnp.zeros(l_s.shape, F32)
        acc_s[...] = jnp.zeros(acc_s.shape, F32)

    def process(get_bias):
        k = k_ref[...].astype(BF16)
        v = v_ref[...].astype(BF16)
        for g in range(grp):
            bias = get_bias(g)
            for j in range(2):
                c0 = g * 2 * hd + j * hd
                q = q_ref[:, c0:c0 + hd].astype(BF16)
                s = _dot_nt(q, k[:, j * hd:(j + 1) * hd]) * scale + bias
                _online_update(g * 2 + j, s, v, m_s, l_s, acc_s)

    @pl.when(ki < qi - 1)
    def _():
        process(lambda g: far_ref[kv * grp + g])

    @pl.when(ki == qi - 1)
    def _():
        process(lambda g: bt_ref[g, 1])

    @pl.when(ki == qi)
    def _():
        process(lambda g: bt_ref[g, 0])
        lam = _lambda(lam_ref, lam_init)
        for g in range(grp):
            o1 = acc_s[g * 2] / l_s[g * 2]
            o2 = acc_s[g * 2 + 1] / l_s[g * 2 + 1]
            o = _sub_norm(o1 - lam * o2, sub_ref[...], lam_init)
            o_ref[:, g * vd:(g + 1) * vd] = o.astype(o_ref.dtype)


def _flash_attention(proj, bias_tiles, bias_far, lam_vecs, subln_w, bsz, seq, lam_init):
    blk = ATTN_BLOCK
    nblk = seq // blk
    grp, hd, vd = ATTN_GROUP, ATTN_HD, ATTN_VD
    qw = grp * 2 * hd
    k_col0 = N_KV_HEADS * qw // (2 * hd)
    v_col0 = k_col0 + N_KV_HEADS
    pairs = [(qi, ki) for qi in range(nblk) for ki in range(qi + 1)]
    q_of = jnp.asarray([p[0] for p in pairs], I32)
    k_of = jnp.asarray([p[1] for p in pairs], I32)
    grid_spec = pltpu.PrefetchScalarGridSpec(
        num_scalar_prefetch=2,
        grid=(bsz, N_KV_HEADS, len(pairs)),
        in_specs=[
            pl.BlockSpec(memory_space=pltpu.SMEM),
            pl.BlockSpec((blk, qw), lambda b, kv, s, qt, kt: (b * nblk + qt[s], kv)),
            pl.BlockSpec((blk, 2 * hd), lambda b, kv, s, qt, kt: (b * nblk + kt[s], k_col0 + kv)),
            pl.BlockSpec((blk, vd), lambda b, kv, s, qt, kt: (b * nblk + kt[s], v_col0 + kv)),
            pl.BlockSpec((grp, 2, blk, blk), lambda b, kv, s, qt, kt: (kv, 0, 0, 0)),
            pl.BlockSpec((4, hd), lambda b, kv, s, qt, kt: (0, 0)),
            pl.BlockSpec((1, vd), lambda b, kv, s, qt, kt: (0, 0)),
        ],
        out_specs=pl.BlockSpec((blk, grp * vd), lambda b, kv, s, qt, kt: (b * nblk + qt[s], kv)),
        scratch_shapes=[pltpu.VMEM((2 * grp, blk, 1), F32), pltpu.VMEM((2 * grp, blk, 1), F32),
                        pltpu.VMEM((2 * grp, blk, vd), F32)],
    )
    return pl.pallas_call(
        functools.partial(_flash_body, scale=hd ** -0.5, lam_init=lam_init),
        grid_spec=grid_spec,
        out_shape=jax.ShapeDtypeStruct((bsz * seq, N_ATTN_HEADS * vd), BF16),
        compiler_params=_cparams(("parallel", "parallel", "arbitrary")),
    )(q_of, k_of, bias_far, proj, proj, proj, bias_tiles, lam_vecs, subln_w)


def _paged_body(pt_ref, qbd_ref, kn_ref, vn_ref, sb_ref, sbs_ref, lam_ref, sub_ref, *rest, n_pages, scale,
                lam_init):
    k_refs = rest[:n_pages]
    v_refs = rest[n_pages:2 * n_pages]
    o_ref = rest[2 * n_pages]
    grp, vd, hd, nkv = ATTN_GROUP, ATTN_VD, ATTN_HD, N_KV_HEADS
    rows = nkv * 2 * grp
    q = qbd_ref[0].astype(BF16)
    q_lo, q_hi = q[:, :hd], q[:, hd:]
    qf = q.astype(F32)
    kn = kn_ref[0].astype(BF16).astype(F32)
    vn = vn_ref[0].astype(BF16).astype(F32)
    kn_rows = jnp.concatenate([jnp.broadcast_to(kn[kv:kv + 1], (2 * grp, 2 * hd)) for kv in range(nkv)], axis=0)
    vn_rows = jnp.concatenate([jnp.broadcast_to(vn[kv:kv + 1], (2 * grp, vd)) for kv in range(nkv)], axis=0)
    s_self = jnp.sum(qf * kn_rows, axis=-1, keepdims=True) * scale + sbs_ref[:, 0:1]

    def flat(ref, c):
        blk = ref[:, :, c * hd:(c + 1) * hd]
        return blk.reshape(blk.shape[0] * blk.shape[1], hd).astype(BF16)

    s_pages = []
    for i in range(n_pages):
        s = _dot_nt(q_lo, flat(k_refs[i], 0)) + _dot_nt(q_hi, flat(k_refs[i], 1))
        s_pages.append(s * scale + sb_ref[i])
    m_elem = s_pages[0]
    for s in s_pages[1:]:
        m_elem = jnp.maximum(m_elem, s)
    m = jnp.maximum(jnp.max(m_elem, axis=-1, keepdims=True), s_self)
    p_self = jnp.exp(s_self - m)
    acc_lo = p_self * vn_rows[:, :hd]
    acc_hi = p_self * vn_rows[:, hd:]
    l_elem = None
    for i in range(n_pages):
        p = jnp.exp(s_pages[i] - m)
        l_elem = p if l_elem is None else l_elem + p
        pb = p.astype(BF16)
        acc_lo = acc_lo + _dot(pb, flat(v_refs[i], 0))
        acc_hi = acc_hi + _dot(pb, flat(v_refs[i], 1))
    l = p_self + jnp.sum(l_elem, axis=-1, keepdims=True)
    n = jnp.concatenate([acc_lo, acc_hi], axis=1) / l
    lam = _lambda(lam_ref, lam_init)
    for kv in range(nkv):
        r0 = kv * 2 * grp
        o = _sub_norm(n[r0:r0 + grp] - lam * n[r0 + grp:r0 + 2 * grp], sub_ref[...], lam_init)
        for g in range(grp):
            c0 = (kv * grp + g) * vd
            o_ref[0, :, c0:c0 + vd] = o[g:g + 1]


def _paged_attention(page_table, qbd, k_new, v_new, cache_k, cache_v, layer, sbias, sbias_self, lam_vecs,
                     subln_w, lam_init):
    bs, n_pages = page_table.shape
    page = cache_k.shape[2]
    kw = 2 * ATTN_HD
    grp, vd = ATTN_GROUP, ATTN_VD

    rows = N_KV_HEADS * 2 * grp

    def page_spec(width, i):
        return pl.BlockSpec((None, None, page, N_KV_HEADS, width), lambda b, pt: (layer, pt[b, i], 0, 0, 0))

    grid_spec = pltpu.PrefetchScalarGridSpec(
        num_scalar_prefetch=1,
        grid=(bs,),
        in_specs=[
            pl.BlockSpec((1, rows, kw), lambda b, pt: (b, 0, 0)),
            pl.BlockSpec((1, N_KV_HEADS, kw), lambda b, pt: (b, 0, 0)),
            pl.BlockSpec((1, N_KV_HEADS, vd), lambda b, pt: (b, 0, 0)),
            pl.BlockSpec((n_pages, rows, page * N_KV_HEADS), lambda b, pt: (0, 0, 0)),
            pl.BlockSpec((rows, LANES), lambda b, pt: (0, 0)),
            pl.BlockSpec((4, ATTN_HD), lambda b, pt: (0, 0)),
            pl.BlockSpec((1, vd), lambda b, pt: (0, 0)),
        ] + [page_spec(kw, i) for i in range(n_pages)] + [page_spec(vd, i) for i in range(n_pages)],
        out_specs=pl.BlockSpec((1, 1, N_ATTN_HEADS * vd), lambda b, pt: (b, 0, 0)),
    )
    return pl.pallas_call(
        functools.partial(_paged_body, n_pages=n_pages, scale=ATTN_HD ** -0.5, lam_init=lam_init),
        grid_spec=grid_spec,
        out_shape=jax.ShapeDtypeStruct((bs, 1, N_ATTN_HEADS * vd), F32),
        compiler_params=_cparams(("parallel",)),
    )(page_table, qbd, k_new, v_new, sbias, sbias_self, lam_vecs, subln_w, *([cache_k] * n_pages),
      *([cache_v] * n_pages))


def _softplus(x):
    return jnp.maximum(x, 0.0) + jnp.log(1.0 + jnp.exp(-jnp.abs(x)))


def _l2norm(x):
    return x * lax.rsqrt(jnp.sum(x * x, -1, keepdims=True) + 1e-6)


def _split_bf16(x):
    hi = x.astype(BF16)
    return hi, (x - hi.astype(F32)).astype(BF16)


def _unit_lower_inverse(a, nilpotency):
    n = a.shape[0]
    eye = (lax.broadcasted_iota(I32, (n, n), 0) == lax.broadcasted_iota(I32, (n, n), 1)).astype(F32)
    y = -a
    r = eye + y
    span = 2
    while span < nilpotency:
        y_hi, y_lo = _split_bf16(y)
        y = _dot(y_hi, y_hi) + (_dot(y_hi, y_lo) + _dot(y_lo, y_hi))
        y_hi, y_lo = _split_bf16(y)
        r_hi, r_lo = _split_bf16(r)
        r = r + (_dot(r_hi, y_hi) + (_dot(r_hi, y_lo) + _dot(r_lo, y_hi)))
        span *= 2
    return r


def _gdn_body(q_ref, k_ref, v_ref, qp_ref, kp_ref, vp_ref, z_ref, ba_ref, cwq_ref, cwk_ref, cwv_ref,
              alog_ref, dtb_ref, nw_ref, o_ref, sout_ref, s_s, *, hg_size, chunk):
    hg = pl.program_id(1)
    c = pl.program_id(2)
    first = c == 0
    dk, dv = GDN_DK, GDN_DV

    @pl.when(first)
    def _():
        s_s[...] = jnp.zeros(s_s.shape, F32)

    def conv_silu(cur_ref, prev_ref, w_ref):
        cur = cur_ref[...]
        prev = jnp.where(first, 0.0, prev_ref[...])
        ext = jnp.concatenate([prev, cur], axis=0)
        y = cur * w_ref[GDN_CONV - 1:GDN_CONV, :]
        for d in range(1, GDN_CONV):
            shifted = pltpu.roll(ext, d, 0)[SUBLANES:SUBLANES + chunk]
            y = y + shifted * w_ref[GDN_CONV - 1 - d:GDN_CONV - d, :]
        return y * _sigmoid(y)

    qc = conv_silu(q_ref, qp_ref, cwq_ref)
    kc = conv_silu(k_ref, kp_ref, cwk_ref)
    vc = conv_silu(v_ref, vp_ref, cwv_ref)

    raw = ba_ref[...]
    off = hg * hg_size
    b_raw = pltpu.roll(raw, (LANES - off) % LANES, 1)
    a_raw = pltpu.roll(raw, (2 * LANES - GDN_HEADS - off) % LANES, 1)
    beta_all = _sigmoid(b_raw)
    g_all = -jnp.exp(alog_ref[0]) * _softplus(a_raw + dtb_ref[0])
    rows = lax.broadcasted_iota(I32, g_all.shape, 0)
    gc_all = g_all
    span = 1
    while span < chunk:
        gc_all = gc_all + jnp.where(rows >= span, pltpu.roll(gc_all, span, 0), 0.0)
        span *= 2
    gc_t = gc_all.T
    eg_all = jnp.exp(gc_all)

    nst = GDN_STACK
    rows_g = nst * chunk
    ri = lax.broadcasted_iota(I32, (rows_g, rows_g), 0)
    ci = lax.broadcasted_iota(I32, (rows_g, rows_g), 1)
    same_head = (ri // chunk) == (ci // chunk)
    causal = same_head & (ri >= ci)
    strict = same_head & (ri > ci)

    def stack(x_all, width, h0):
        return jnp.concatenate([x_all[:, (h0 + s) * width:(h0 + s + 1) * width] for s in range(nst)], axis=0)

    z_all = z_ref[...]
    for h0 in range(0, hg_size, nst):
        q = _l2norm(stack(qc, dk, h0)) * dk ** -0.5
        k = _l2norm(stack(kc, dk, h0))
        v = stack(vc, dv, h0)
        beta = stack(beta_all, 1, h0)
        gcol = stack(gc_all, 1, h0)
        egc = stack(eg_all, 1, h0)
        grow = jnp.concatenate([gc_t[h0 + s:h0 + s + 1, :] for s in range(nst)], axis=1)
        decay = jnp.where(causal, jnp.exp(jnp.where(causal, gcol - grow, 0.0)), 0.0)
        kb = k * beta
        kbf = k.astype(BF16)
        p = _dot_nt(jnp.concatenate([kb, q], axis=0).astype(BF16), kbf)
        a_mat = jnp.where(strict, p[:rows_g] * decay, 0.0)
        qk = p[rows_g:] * decay
        t_mat = _unit_lower_inverse(a_mat, chunk).astype(BF16)
        uw = _dot(t_mat, jnp.concatenate([v * beta, kb * egc], axis=1).astype(BF16))
        u, w = uw[:, :dv], uw[:, dv:]
        qd = q * egc
        ws, qs = [], []
        for s in range(nst):
            r0, r1 = s * chunk, (s + 1) * chunk
            both = _dot(jnp.concatenate([w[r0:r1], qd[r0:r1]], axis=0).astype(BF16), s_s[h0 + s].astype(BF16))
            ws.append(both[:chunk])
            qs.append(both[chunk:])
        v_new = u - jnp.concatenate(ws, axis=0)
        v_new_bf = v_new.astype(BF16)
        o = jnp.concatenate(qs, axis=0) + _dot(qk.astype(BF16), v_new_bf)
        for s in range(nst):
            r0, r1 = s * chunk, (s + 1) * chunk
            g_last = gcol[r1 - 1:r1, :]
            k_dec = (k[r0:r1] * jnp.exp(g_last - gcol[r0:r1])).astype(BF16)
            s_s[h0 + s] = s_s[h0 + s] * jnp.exp(g_last) + _dot_tn(k_dec, v_new_bf[r0:r1])
        z = stack(z_all, dv, h0)
        on = o * lax.rsqrt(jnp.mean(o * o, -1, keepdims=True) + 1e-6) * nw_ref[...] * (z * _sigmoid(z))
        for s in range(nst):
            o_ref[:, (h0 + s) * dv:(h0 + s + 1) * dv] = on[s * chunk:(s + 1) * chunk].astype(o_ref.dtype)

    @pl.when(c == pl.num_programs(2) - 1)
    def _():
        sout_ref[0] = s_s[...]


def _gdn_prompt(proj, ba, conv_w, alog_g, dtb_g, norm_w, bsz, seq, col0):
    hgs, chunk = GDN_HEAD_GROUP, GDN_CHUNK
    n_hg = GDN_HEADS // hgs
    w = hgs * GDN_DK
    n_chunks = seq // chunk
    hw = GDN_HEADS * GDN_DK
    assert col0 % w == 0 and seq % chunk == 0
    qb, kb, vb, zb = (col0 // w + i * (hw // w) for i in range(4))
    cpt = chunk // SUBLANES

    def cur(base):
        return pl.BlockSpec((chunk, w), lambda b, hg, c: (b * n_chunks + c, base + hg))

    def prev(base):
        return pl.BlockSpec((SUBLANES, w),
                            lambda b, hg, c: (jnp.maximum((b * n_chunks + c) * cpt - 1, 0), base + hg))

    def cw(base):
        return pl.BlockSpec((GDN_CONV, w), lambda b, hg, c: (0, base + hg))

    return pl.pallas_call(
        functools.partial(_gdn_body, hg_size=hgs, chunk=chunk),
        grid=(bsz, n_hg, n_chunks),
        in_specs=[cur(qb), cur(kb), cur(vb), prev(qb), prev(kb), prev(vb), cur(zb),
                  pl.BlockSpec((chunk, LANES), lambda b, hg, c: (b * n_chunks + c, 0)),
                  cw(0), cw(hw // w), cw(2 * hw // w),
                  pl.BlockSpec((1, 1, LANES), lambda b, hg, c: (hg, 0, 0)),
                  pl.BlockSpec((1, 1, LANES), lambda b, hg, c: (hg, 0, 0)),
                  pl.BlockSpec((1, GDN_DV), lambda b, hg, c: (0, 0))],
        out_specs=[pl.BlockSpec((chunk, w), lambda b, hg, c: (b * n_chunks + c, hg)),
                   pl.BlockSpec((1, hgs, GDN_DK, GDN_DV), lambda b, hg, c: (b, hg, 0, 0))],
        out_shape=[jax.ShapeDtypeStruct((bsz * seq, hw), BF16),
                   jax.ShapeDtypeStruct((bsz, GDN_HEADS, GDN_DK, GDN_DV), F32)],
        scratch_shapes=[pltpu.VMEM((hgs, GDN_DK, GDN_DV), F32)],
        compiler_params=_cparams(("parallel", "parallel", "arbitrary")),
    )(proj, proj, proj, proj, proj, proj, proj, ba, conv_w, conv_w, conv_w, alog_g, dtb_g, norm_w)


def _gdn_step_body(new_ref, buf_ref, cw_ref, z_ref, b_ref, a_ref, alog_ref, dtb_ref, nw_ref, s_ref,
                   o_ref, sout_ref, cout_ref):
    nh = GDN_HEADS
    new = new_ref[0]
    y = new * cw_ref[GDN_CONV - 1]
    for j in range(GDN_CONV - 1):
        y = y + buf_ref[0, j] * cw_ref[j]
        if j > 0:
            cout_ref[0, j - 1] = buf_ref[0, j]
    cout_ref[0, GDN_CONV - 2] = new
    y = y * _sigmoid(y)
    q = _l2norm(y[0:nh]) * GDN_DK ** -0.5
    k = _l2norm(y[nh:2 * nh])
    v = y[2 * nh:3 * nh]
    beta = _sigmoid(b_ref[0])
    eg = jnp.exp(-jnp.exp(alog_ref[...]) * _softplus(a_ref[0] + dtb_ref[...]))
    q_t = q.T
    k_t = k.T
    z = z_ref[0]
    for h in range(nh):
        s_old = s_ref[0, h]
        kcol = k_t[:, h:h + 1]
        egh = eg[h:h + 1, :]
        sk = jnp.sum(kcol * s_old, axis=0, keepdims=True)
        v_new = beta[h:h + 1, :] * (v[h:h + 1, :] - egh * sk)
        s_new = s_old * egh + kcol * v_new
        sout_ref[0, h] = s_new
        o = jnp.sum(q_t[:, h:h + 1] * s_new, axis=0, keepdims=True)
        zh = z[h:h + 1, :]
        on = o * lax.rsqrt(jnp.mean(o * o, -1, keepdims=True) + 1e-6) * nw_ref[...] * (zh * _sigmoid(zh))
        o_ref[0, h:h + 1, :] = on


def _gdn_step(qkv_new, conv_buf, conv_w, z, b_in, a_in, a_log, dt_bias, norm_w, state):
    bs = qkv_new.shape[0]
    nh, nr = GDN_HEADS, 3 * GDN_HEADS
    per_b3 = lambda shape: pl.BlockSpec((1,) + shape, lambda b: (b, 0, 0))
    per_b4 = lambda shape: pl.BlockSpec((1,) + shape, lambda b: (b, 0, 0, 0))
    return pl.pallas_call(
        _gdn_step_body,
        grid=(bs,),
        in_specs=[per_b3((nr, GDN_DK)), per_b4((GDN_CONV - 1, nr, GDN_DK)),
                  pl.BlockSpec((GDN_CONV, nr, GDN_DK), lambda b: (0, 0, 0)),
                  per_b3((nh, GDN_DV)), per_b3((nh, 1)), per_b3((nh, 1)),
                  pl.BlockSpec((nh, 1), lambda b: (0, 0)), pl.BlockSpec((nh, 1), lambda b: (0, 0)),
                  pl.BlockSpec((1, GDN_DV), lambda b: (0, 0)),
                  per_b4((nh, GDN_DK, GDN_DV))],
        out_specs=[per_b3((nh, GDN_DV)), per_b4((nh, GDN_DK, GDN_DV)), per_b4((GDN_CONV - 1, nr, GDN_DK))],
        out_shape=[jax.ShapeDtypeStruct((bs, nh, GDN_DV), F32),
                   jax.ShapeDtypeStruct((bs, nh, GDN_DK, GDN_DV), F32),
                   jax.ShapeDtypeStruct((bs, GDN_CONV - 1, nr, GDN_DK), F32)],
        compiler_params=_cparams(("parallel",)),
    )(qkv_new, conv_buf, conv_w, z, b_in, a_in, a_log, dt_bias, norm_w, state)


def _lane_partner(x, lane, s):
    return jnp.where((lane & s) != 0, pltpu.roll(x, s, 1), pltpu.roll(x, LANES - s, 1))


def _group_reduce(x, lane, op):
    s = 1
    while s < GROUP_SIZE:
        x = op(x, _lane_partner(x, lane, s))
        s *= 2
    return x


def _router_body(hp_ref, rw_ref, rb_ref, idx_ref, w_ref, rank_ref, cnt_ref, carry_s):
    i = pl.program_id(0)

    @pl.when(i == 0)
    def _():
        carry_s[...] = jnp.zeros(carry_s.shape, F32)

    half = rw_ref.shape[0] // 2
    lo, hi = _unpack_bf16_pairs(_load_slabs(hp_ref, ROW_TILE, half // LANES))
    rw = rw_ref[...].astype(BF16)
    scores = _sigmoid(_dot(lo, rw[:half]) + _dot(hi, rw[half:]))
    choice = scores + rb_ref[...]
    tm = scores.shape[0]
    lane = lax.broadcasted_iota(I32, (tm, LANES), 1)
    grp = lane // GROUP_SIZE
    big = jnp.int32(2 * LANES)

    m1 = _group_reduce(choice, lane, jnp.maximum)
    first = _group_reduce(jnp.where(choice == m1, lane, big), lane, jnp.minimum)
    m2 = _group_reduce(jnp.where(lane == first, NEG_INF, choice), lane, jnp.maximum)
    gs = m1 + m2
    beaten = jnp.zeros((tm, LANES), I32)
    for d in range(1, N_GROUPS):
        other = pltpu.roll(gs, d * GROUP_SIZE, 1)
        other_grp = pltpu.roll(grp, d * GROUP_SIZE, 1)
        beats = (other > gs) | ((other == gs) & (other_grp < grp))
        beaten = beaten + beats.astype(I32)
    masked = jnp.where(beaten < TOPK_GROUPS, choice, NEG_INF)

    idx_out = jnp.zeros((tm, LANES), I32)
    w_out = jnp.zeros((tm, LANES), F32)
    onehot = jnp.zeros((tm, LANES), F32)
    sels = []
    for k in range(TOP_K):
        m = jnp.max(masked, axis=-1, keepdims=True)
        idx = jnp.min(jnp.where(masked == m, lane, big), axis=-1, keepdims=True)
        sel = lane == idx
        sels.append(sel)
        wk = jnp.sum(jnp.where(sel, scores, 0.0), axis=-1, keepdims=True)
        idx_out = jnp.where(lane == k, idx, idx_out)
        w_out = jnp.where(lane == k, wk, w_out)
        onehot = jnp.where(sel, 1.0, onehot)
        masked = jnp.where(sel, NEG_INF, masked)
    w_out = w_out / jnp.sum(w_out, axis=-1, keepdims=True) * ROUTED_SCALE

    ri = lax.broadcasted_iota(I32, (tm, tm), 0)
    ci = lax.broadcasted_iota(I32, (tm, tm), 1)
    before = _dot((ri > ci).astype(BF16), onehot.astype(BF16)) + carry_s[...]
    rank_out = jnp.zeros((tm, LANES), F32)
    for k in range(TOP_K):
        rk = jnp.sum(jnp.where(sels[k], before, 0.0), axis=-1, keepdims=True)
        rank_out = jnp.where(lane == k, rk, rank_out)
    carry_s[...] = carry_s[...] + jnp.sum(onehot, axis=0, keepdims=True)

    idx_ref[...] = idx_out
    w_ref[...] = w_out
    rank_ref[...] = rank_out.astype(I32)
    cnt_ref[...] = jnp.broadcast_to(carry_s[...], cnt_ref.shape).astype(I32)


def _router(h_slabs, router_w, router_bias):
    d = router_w.shape[0]
    pitch = d // 2 // LANES
    t = h_slabs.shape[0] // pitch
    row = pl.BlockSpec((ROW_TILE, LANES), lambda i: (i, 0))
    return pl.pallas_call(
        _router_body,
        grid=(t // ROW_TILE,),
        in_specs=[pl.BlockSpec((ROW_TILE * pitch, LANES), lambda i: (i, 0)),
                  pl.BlockSpec((d, N_EXPERTS), lambda i: (0, 0)),
                  pl.BlockSpec((1, N_EXPERTS), lambda i: (0, 0))],
        out_specs=[row, row, row, pl.BlockSpec((SUBLANES, LANES), lambda i: (0, 0))],
        out_shape=[jax.ShapeDtypeStruct((t, LANES), I32), jax.ShapeDtypeStruct((t, LANES), F32),
                   jax.ShapeDtypeStruct((t, LANES), I32), jax.ShapeDtypeStruct((SUBLANES, LANES), I32)],
        scratch_shapes=[pltpu.VMEM((1, LANES), F32)],
        compiler_params=_cparams(("arbitrary",)),
    )(h_slabs, router_w, router_bias)


def _gather_body(used_ref, idx_ref, tab_ref, out_ref, sem, *, rows, n, src_pitch):
    def slab_copy(src, dst):
        return pltpu.make_async_copy(tab_ref.at[pl.ds(src * src_pitch, n)],
                                     out_ref.at[pl.ds(pl.multiple_of(dst * n, n), n)], sem)

    def issue(pair, carry):
        for lane in range(2):
            r = 2 * pair + lane
            slab_copy(idx_ref[0, 0, r], r).start(priority=lane)
        return carry

    def drain(r, carry):
        slab_copy(0, 0).wait()
        return carry

    in_use = pl.program_id(0) * rows < used_ref[0]

    @pl.when(in_use)
    def _():
        lax.fori_loop(0, rows // 2, issue, 0)
        lax.fori_loop(0, rows, drain, 0)

    @pl.when(jnp.logical_not(in_use))
    def _():
        out_ref[...] = jnp.zeros(out_ref.shape, out_ref.dtype)


def _slab_gather(table, idx, n, n_used, src_pitch=None):
    src_pitch = n if src_pitch is None else src_pitch
    count = idx.shape[0]
    rows = GATHER_ROWS
    assert count % rows == 0 and rows % 2 == 0
    return pl.pallas_call(
        functools.partial(_gather_body, rows=rows, n=n, src_pitch=src_pitch),
        grid_spec=pltpu.PrefetchScalarGridSpec(
            num_scalar_prefetch=1,
            grid=(count // rows,),
            in_specs=[pl.BlockSpec((1, 1, rows), lambda i, used: (i, 0, 0), memory_space=pltpu.SMEM),
                      pl.BlockSpec(memory_space=pl.ANY)],
            out_specs=pl.BlockSpec((rows * n, LANES), lambda i, used: (i, 0)),
            scratch_shapes=[pltpu.SemaphoreType.DMA(())],
        ),
        out_shape=jax.ShapeDtypeStruct((count * n, LANES), table.dtype),
        compiler_params=_cparams(("arbitrary",)),
    )(jnp.asarray(n_used, I32).reshape(1), idx.reshape(count // rows, 1, rows), table)


def _ffn_up_body(be_ref, bv_ref, x_ref, wg_ref, wu_ref, o_ref):
    i = pl.program_id(0)

    @pl.when(bv_ref[i] > 0)
    def _():
        half = wg_ref.shape[0] // 2
        lo, hi = _unpack_bf16_pairs(_load_slabs(x_ref, o_ref.shape[0], half // LANES))
        wg = wg_ref[...].astype(BF16)
        wu = wu_ref[...].astype(BF16)
        g = _dot(lo, wg[:half]) + _dot(hi, wg[half:])
        u = _dot(lo, wu[:half]) + _dot(hi, wu[half:])
        o_ref[...] = (g * _sigmoid(g) * u).astype(o_ref.dtype)

    @pl.when(bv_ref[i] == 0)
    def _():
        o_ref[...] = jnp.zeros(o_ref.shape, o_ref.dtype)


def _ffn_down_body(be_ref, bv_ref, a_ref, wd_ref, o_ref, *, slab_pitch, k_chunk):
    i = pl.program_id(1)

    @pl.when(bv_ref[i] > 0)
    def _():
        a = a_ref[...]
        acc = None
        for c0 in range(0, a.shape[1], k_chunk):
            part = _dot(a[:, c0:c0 + k_chunk], wd_ref[c0:c0 + k_chunk, :].astype(BF16))
            acc = part if acc is None else acc + part
        if slab_pitch:
            _store_slabs(o_ref, acc, slab_pitch)
        else:
            o_ref[...] = acc

    @pl.when(bv_ref[i] == 0)
    def _():
        o_ref[...] = jnp.zeros(o_ref.shape, o_ref.dtype)


def _ffn_up(x_slabs, blocks, w_gate, w_up, layer, tm, tf):
    block_e, block_valid, block_row = blocks
    d = w_gate.shape[-2]
    ff = w_gate.shape[-1]
    pitch = d // 2 // LANES
    n_rows = x_slabs.shape[0] // pitch
    nb = block_e.shape[0]
    nj = ff // tf

    def w_index(i, j, be, bv):
        return layer, be[i], 0, jnp.where(bv[i] > 0, j, nj - 1)

    up = pl.pallas_call(
        _ffn_up_body,
        grid_spec=pltpu.PrefetchScalarGridSpec(
            num_scalar_prefetch=2,
            grid=(nb, nj),
            in_specs=[pl.BlockSpec((tm * pitch, LANES), lambda i, j, be, bv: (bv[nb + i], 0)),
                      pl.BlockSpec((None, None, d, tf), w_index),
                      pl.BlockSpec((None, None, d, tf), w_index)],
            out_specs=pl.BlockSpec((tm, tf), lambda i, j, be, bv: (i, j)),
        ),
        out_shape=jax.ShapeDtypeStruct((n_rows, ff), BF16),
        compiler_params=_cparams(("arbitrary", "arbitrary")),
    )
    return up(block_e, jnp.concatenate([block_valid, block_row]), x_slabs, w_gate, w_up)


def _ffn_down(act, blocks, w_down, layer, tm, tn, slab_pitch=0):
    block_e, block_valid, block_row = blocks
    n_rows, ff = act.shape
    d = w_down.shape[-1]
    nb = block_e.shape[0]
    assert not slab_pitch or tn == d
    bv = jnp.concatenate([block_valid, block_row])
    if slab_pitch:
        out_spec = pl.BlockSpec((tm * slab_pitch, LANES), lambda j, i, be, bv: (i, 0))
        out_shape = jax.ShapeDtypeStruct((n_rows * slab_pitch, LANES), F32)
    else:
        out_spec = pl.BlockSpec((tm, tn), lambda j, i, be, bv: (i, j))
        out_shape = jax.ShapeDtypeStruct((n_rows, d), F32)
    down = pl.pallas_call(
        functools.partial(_ffn_down_body, slab_pitch=slab_pitch, k_chunk=min(ff, 256)),
        grid_spec=pltpu.PrefetchScalarGridSpec(
            num_scalar_prefetch=2,
            grid=(d // tn, nb),
            in_specs=[pl.BlockSpec((tm, ff), lambda j, i, be, bv: (bv[nb + i], 0)),
                      pl.BlockSpec((None, None, ff, tn), lambda j, i, be, bv: (layer, be[i], 0, j))],
            out_specs=out_spec,
        ),
        out_shape=out_shape,
        compiler_params=_cparams(("arbitrary", "arbitrary")),
    )
    return down(block_e, bv, act, w_down)


def _moe(h_slabs, p, layer, t):
    d = p["router_w"].shape[0]
    top_idx, top_w, rank, counts = _router(h_slabs, p["router_w"], p["router_bias"])
    tm = MOE_ROW_BLOCK
    counts = counts[0]
    padded = (counts + tm - 1) // tm * tm
    pad_end = jnp.cumsum(padded)
    pad_start = pad_end - padded
    experts = jnp.arange(N_EXPERTS, dtype=I32)
    start_of = jnp.sum(jnp.where(top_idx[:, :TOP_K, None] == experts, pad_start, 0), axis=-1)
    pos = (start_of + rank[:, :TOP_K]).reshape(-1)
    n_blocks = (t * TOP_K) // tm + N_EXPERTS
    while (n_blocks * tm) % GATHER_ROWS:
        n_blocks += 1
    n_slots = n_blocks * tm
    slot_tok = jnp.zeros((n_slots,), I32).at[pos].set(jnp.repeat(jnp.arange(t, dtype=I32), TOP_K))
    n_used_rows = pad_end[-1]

    def blocks_of(rows_per_block):
        nb = n_blocks * (tm // rows_per_block)
        starts = jnp.arange(nb, dtype=I32) * rows_per_block
        blk = jnp.minimum(jnp.arange(nb, dtype=I32), n_used_rows // rows_per_block - 1)
        e = jnp.minimum(jnp.searchsorted(pad_end, blk * rows_per_block, side="right"), N_EXPERTS - 1).astype(I32)
        real_end = (pad_start + counts)[e]
        valid = ((starts < n_used_rows) & (starts < real_end)).astype(I32)
        return e, valid, blk

    x_sorted = _slab_gather(h_slabs, slot_tok, d // 2 // LANES, n_used_rows)
    y_pitch = d // LANES + Y_SLAB_PAD
    act = _ffn_up(x_sorted, blocks_of(tm), p["exp_gate"], p["exp_up"], layer, tm=tm,
                  tf=min(256, p["exp_gate"].shape[-1]))
    dtm = tm // 2
    y_sorted = _ffn_down(act, blocks_of(dtm), p["exp_down"], layer, tm=dtm, tn=d, slab_pitch=y_pitch)
    y_tok = _slab_gather(y_sorted, pos, d // LANES, t * TOP_K, src_pitch=y_pitch)

    stm = _pick_tm(t, 1024)
    nsb = t // stm
    one = jnp.ones((nsb,), I32)
    sblocks = (0 * one, one, jnp.arange(nsb, dtype=I32))
    lead = lambda w: w.reshape((w.shape[0], 1) + w.shape[1:])
    sact = _ffn_up(h_slabs, sblocks, lead(p["sh_gate"]), lead(p["sh_up"]), layer, tm=stm, tf=256)
    shared = _ffn_down(sact, sblocks, lead(p["sh_down"]), layer, tm=stm, tn=1024)
    return y_tok, top_w, shared


def kernel(x_prompt, x_sample, cache_k, cache_v, state_gdn, state_conv, page_table, c_prompt, c_sample, rel_bias, ada_w, ada_b, w_in, lam_q1, lam_k1, lam_q2, lam_k2, subln_w, conv_w, a_log, dt_bias, gdn_norm_w, w_branch, w_o, ln1_g, ln1_b, router_w, router_bias, exp_gate, exp_up, exp_down, sh_gate, sh_up, sh_down, ln2_g, ln2_b):
    bsz, seq, d = x_prompt.shape
    bs = x_sample.shape[0]
    depth = ada_w.shape[0]
    tp = bsz * seq
    t = tp + bs
    n_pages = page_table.shape[1]
    page = cache_k.shape[2]
    past = n_pages * page
    alpha = (2 * depth) ** 0.25
    assert bs == ROW_TILE and seq % ATTN_BLOCK == 0 and x_sample.shape[1] == 1

    x = jnp.concatenate([x_prompt.reshape(tp, d), x_sample.reshape(bs, d)], axis=0)
    c_all = jnp.concatenate([c_prompt, c_sample, jnp.zeros((-(bsz + bs) % SUBLANES, d), F32)], axis=0)

    q_w = N_ATTN_HEADS * 2 * ATTN_HD
    kv_w = N_KV_HEADS * 2 * ATTN_HD
    hw = GDN_HEADS * GDN_DK
    gdn0 = q_w + 2 * kv_w
    main_w = gdn0 + 4 * hw
    ba_w = 2 * GDN_HEADS

    blk = ATTN_BLOCK
    rel_line = jnp.arange(-(blk - 1), 2 * blk)
    line = jnp.where(rel_line >= 0, rel_bias[_t5_bucket(rel_line)].astype(F32).T, NEG_INF)
    bias_tiles = jnp.stack([_toeplitz(line[:, :2 * blk - 1]), _toeplitz(line[:, blk:])], axis=1)
    bias_far = rel_bias[N_BUCKETS - 1].astype(F32)
    rel_dec = past - jnp.arange(past)
    dec = rel_bias[_t5_bucket(rel_dec)].astype(F32).reshape(n_pages, page, N_KV_HEADS, ATTN_GROUP)
    dec = jnp.transpose(dec, (0, 2, 3, 1))[:, :, None, :, :, None]
    same_kv = jnp.eye(N_KV_HEADS, dtype=bool)[None, :, None, None, None, :]
    sbias = jnp.where(same_kv, jnp.broadcast_to(dec, (n_pages, N_KV_HEADS, 2, ATTN_GROUP, page, N_KV_HEADS)),
                      NEG_INF).reshape(n_pages, N_KV_HEADS * 2 * ATTN_GROUP, page * N_KV_HEADS)
    self_b = rel_bias[0].astype(F32).reshape(N_KV_HEADS, 1, ATTN_GROUP)
    sbias_self = jnp.broadcast_to(jnp.broadcast_to(self_b, (N_KV_HEADS, 2, ATTN_GROUP)).reshape(
        N_KV_HEADS * 2 * ATTN_GROUP, 1), (N_KV_HEADS * 2 * ATTN_GROUP, LANES))

    y = x
    outs = {n: [] for n in ("kp", "vp", "sp", "cp", "ks", "vs", "ss", "cs")}
    for l in range(depth):
        lam_init = 0.8 - 0.6 * math.exp(-0.3 * l)
        p = {"router_w": router_w[l], "router_bias": router_bias[l].reshape(1, -1), "exp_gate": exp_gate,
             "exp_up": exp_up, "exp_down": exp_down, "sh_gate": sh_gate, "sh_up": sh_up, "sh_down": sh_down}

        mod = _mm(c_all, ada_w[l], tm=c_all.shape[0], tn=512, n_out=6 * d, bias=ada_b[l].reshape(1, -1),
                  a_silu=True)
        mod_p = mod[:bsz].reshape(bsz, 1, 6 * d)
        mod_s = mod[bsz:bsz + bs]

        h = _modulate(y, mod_p, mod_s, 1, 0, seq)
        tm_big = _pick_tm(t, 1664)
        proj = _mm(h, w_in[l], tm=tm_big, tn=256, n_out=main_w)
        w_tail = w_in[l][:, main_w:]
        w_ba = jnp.pad(w_tail[:, :ba_w], ((0, 0), (0, LANES - ba_w)))
        ba = _mm(h, w_ba, tm=tm_big, tn=LANES, n_out=LANES)
        gates = _mm(h, w_tail[:, ba_w:], tm=tm_big, tn=256, n_out=2 * d)

        lam_vecs = jnp.stack([lam_q1[l], lam_k1[l], lam_q2[l], lam_k2[l]]).astype(F32)
        sub_w = subln_w[l].reshape(1, -1).astype(F32)

        o_a_p = _flash_attention(proj, bias_tiles, bias_far, lam_vecs, sub_w, bsz, seq, lam_init)
        srow = proj[tp:]
        q_s = srow[:, :q_w].reshape(bs, N_KV_HEADS, ATTN_GROUP, 2, ATTN_HD)
        zeros = jnp.zeros_like(q_s[:, :, :, 0])
        qbd = jnp.stack([jnp.concatenate([q_s[:, :, :, 0], zeros], -1),
                         jnp.concatenate([zeros, q_s[:, :, :, 1]], -1)], axis=2)
        qbd = qbd.reshape(bs, N_KV_HEADS * 2 * ATTN_GROUP, 2 * ATTN_HD)
        k_s = srow[:, q_w:q_w + kv_w]
        v_s = srow[:, q_w + kv_w:gdn0]
        o_a_s = _paged_attention(page_table, qbd, k_s.reshape(bs, N_KV_HEADS, 2 * ATTN_HD),
                                 v_s.reshape(bs, N_KV_HEADS, ATTN_VD), cache_k, cache_v, l, sbias,
                                 sbias_self, lam_vecs, sub_w, lam_init)
        o_a = jnp.concatenate([o_a_p, o_a_s.reshape(bs, -1).astype(BF16)], axis=0)

        n_hg = GDN_HEADS // GDN_HEAD_GROUP
        group_lanes = lambda vec: jnp.pad(vec.astype(F32).reshape(n_hg, 1, GDN_HEAD_GROUP),
                                          ((0, 0), (0, 0), (0, LANES - GDN_HEAD_GROUP)))
        nw = gdn_norm_w[l].reshape(1, -1).astype(F32)
        o_g_p, s_p = _gdn_prompt(proj, ba, conv_w[l], group_lanes(a_log[l]), group_lanes(dt_bias[l]), nw,
                                 bsz, seq, gdn0)
        nr = 3 * GDN_HEADS
        o_g_s, s_s, c_s = _gdn_step(
            srow[:, gdn0:gdn0 + 3 * hw].reshape(bs, nr, GDN_DK),
            state_conv[l].reshape(bs, GDN_CONV - 1, nr, GDN_DK),
            conv_w[l].reshape(GDN_CONV, nr, GDN_DK),
            srow[:, gdn0 + 3 * hw:main_w].reshape(bs, GDN_HEADS, GDN_DV),
            ba[tp:, :GDN_HEADS].reshape(bs, GDN_HEADS, 1), ba[tp:, GDN_HEADS:ba_w].reshape(bs, GDN_HEADS, 1),
            a_log[l].reshape(-1, 1).astype(F32), dt_bias[l].reshape(-1, 1).astype(F32), nw, state_gdn[l])
        o_g = jnp.concatenate([o_g_p, o_g_s.reshape(bs, -1).astype(BF16)], axis=0)

        tm_mid = _pick_tm(t, 1024)
        m_a = _mm(o_a, w_branch[l], tm=tm_mid, tn=512, n_out=d, gate=gates)
        merged = _mm(o_g, w_branch[l], tm=tm_mid, tn=512, n_out=d, w_row_blk=1, gate=gates,
                     gate_col_blk=d // 512, prev=m_a, out_dtype=BF16)
        attn_out = _mm(merged, w_o[l], tm=tm_mid, tn=512, n_out=d)
        x1, h2p = _ln1(y, attn_out, mod_p, mod_s, ln1_g[l].reshape(1, -1), ln1_b[l].reshape(1, -1), seq, alpha)

        y_tok, top_w, shared = _moe(h2p, p, l, t)
        y = _final(x1, y_tok, top_w, shared, mod_p, mod_s, ln2_g[l].reshape(1, -1), ln2_b[l].reshape(1, -1),
                   seq, alpha)

        kcols = proj[:, q_w:q_w + kv_w]
        vcols = proj[:, q_w + kv_w:gdn0]
        outs["kp"].append(kcols[:tp].reshape(bsz, seq, N_KV_HEADS, 2 * ATTN_HD))
        outs["vp"].append(vcols[:tp].reshape(bsz, seq, N_KV_HEADS, ATTN_VD))
        outs["sp"].append(s_p)
        tail = jnp.stack([proj[(b + 1) * seq - (GDN_CONV - 1):(b + 1) * seq, gdn0:gdn0 + 3 * hw]
                          for b in range(bsz)])
        outs["cp"].append(tail)
        outs["ks"].append(kcols[tp:].reshape(bs, 1, N_KV_HEADS, 2 * ATTN_HD))
        outs["vs"].append(vcols[tp:].reshape(bs, 1, N_KV_HEADS, ATTN_VD))
        outs["ss"].append(s_s)
        outs["cs"].append(c_s.reshape(bs, GDN_CONV - 1, 3 * hw))

    st = lambda n: jnp.stack(outs[n])
    return (y[:tp].reshape(bsz, seq, d), y[tp:].reshape(bs, 1, d), st("kp"), st("vp"), st("sp"), st("cp"),
            st("ks"), st("vs"), st("ss"), st("cs"))
```
